```python
import math
import jax, jax.numpy as jnp
from jax import lax
import numpy as np


D_MODEL = 1024
BATCH = 32
SEQ = 2048
DEPTH = 4
DEC_BATCH = 8
DEC_SEQ = 64
PAST_LEN = 4096

CHUNK = 64
N_MIXERS = 2
EXPAND = 2
D_INNER = EXPAND * D_MODEL
MEM_LEN = 256
MEM_HEADS = 4
MEM_HD = 128
MEM_W = MEM_HEADS * MEM_HD
SB_HEADS = 12
SB_HD = 128
SB_W = SB_HEADS * SB_HD
SB_BLOCK = 128
SWA_HEADS = 24
SWA_KV_HEADS = 3
SWA_GROUP = SWA_HEADS // SWA_KV_HEADS
SWA_HD = 64
SWA_W = SWA_HEADS * SWA_HD
SWA_KV_W = SWA_KV_HEADS * SWA_HD
WINDOW = 128
NUM_BUCKETS = 32
MAX_DISTANCE = 128
EPS = 1e-6
NEG_INF = -1e30
N_A = (DEPTH + 1) // 2
N_B = DEPTH // 2
IN_A = 3 * SB_W + MEM_W + D_INNER
IN_B = SWA_W + 2 * SWA_KV_W + MEM_W + D_INNER

kernel_name = 'hybrid_stickbreak_swa_stream_step'


def rmsnorm(x, g):
    xf = x.astype(jnp.float32)
    y = xf * lax.rsqrt(jnp.mean(xf * xf, axis=-1, keepdims=True) + EPS)
    return (y * g.astype(jnp.float32)).astype(x.dtype)


def split_cols(t, widths):
    idx = np.cumsum(widths)[:-1].tolist()
    return jnp.split(t, idx, axis=-1)


def t5_bucket(rel):
    nb = NUM_BUCKETS // 2
    max_exact = nb // 2
    ret = jnp.where(rel > 0, nb, 0)
    n = jnp.abs(rel)
    nf = jnp.maximum(n, 1).astype(jnp.float32)
    large = max_exact + (jnp.log(nf / max_exact) / math.log(MAX_DISTANCE / max_exact)
                         * (nb - max_exact)).astype(jnp.int32)
    large = jnp.minimum(large, nb - 1)
    return ret + jnp.where(n < max_exact, n, large)


def band_bias(rel_bias, n_q, n_k):
    rel = jnp.arange(n_k)[None, :] - WINDOW - jnp.arange(n_q)[:, None]
    b = jnp.take(rel_bias, t5_bucket(rel), axis=0)
    b = jnp.transpose(b, (2, 0, 1)).reshape(SWA_KV_HEADS, SWA_GROUP, n_q, n_k)
    return b.astype(jnp.float32)


def sb_attend(q, k, v, q_pos, k_pos):
    z = jnp.einsum('bqhd,bkhd->bhqk', q, k).astype(jnp.float32) * SB_HD ** -0.5
    causal = k_pos[None, :] < q_pos[:, None]
    log_1m = jnp.where(causal, jax.nn.log_sigmoid(-z), 0.0)
    suffix = lax.cumsum(log_1m, axis=3, reverse=True) - log_1m
    a = jnp.where(causal, jnp.exp(jax.nn.log_sigmoid(z) + suffix), 0.0)
    return jnp.einsum('bhqk,bkhd->bqhd', a.astype(v.dtype), v)


def sb_prompt(q, k, v):
    B, S = q.shape[:2]
    nb = S // SB_BLOCK
    qb = q.reshape(B, nb, SB_BLOCK, SB_HEADS, SB_HD).transpose(1, 0, 2, 3, 4)
    qpos = jnp.arange(S).reshape(nb, SB_BLOCK)
    kpos = jnp.arange(S)
    ob = lax.map(lambda a: sb_attend(a[0], k, v, a[1], kpos), (qb, qpos))
    return ob.transpose(1, 0, 2, 3, 4).reshape(B, S, SB_W)


def sb_sample(q, k_new, v_new, k_cache, v_cache):
    B, T = q.shape[:2]
    P = k_cache.shape[1]
    k = jnp.concatenate([k_cache, k_new], axis=1)
    v = jnp.concatenate([v_cache, v_new], axis=1)
    o = sb_attend(q, k, v, P + jnp.arange(T), jnp.arange(P + T))
    return o.reshape(B, T, SB_W)


def swa_core(q, k, v, bias, key_valid, sink):
    s = jnp.einsum('bnqhgd,bnkhd->bnhgqk', q, k).astype(jnp.float32) * SWA_HD ** -0.5 + bias
    s = jnp.where(key_valid[None, :, None, None, None, :], s, NEG_INF)
    sk = sink.astype(jnp.float32).reshape(1, 1, SWA_KV_HEADS, SWA_GROUP, 1)
    m = jnp.maximum(jnp.max(s, axis=-1), sk)
    p = jnp.exp(s - m[..., None])
    den = jnp.sum(p, axis=-1) + jnp.exp(sk - m)
    p = p / den[..., None]
    return jnp.einsum('bnhgqk,bnkhd->bnqhgd', p.astype(v.dtype), v)


def swa_prompt(q, k, v, bias, sink):
    B, S = q.shape[:2]
    nc = S // CHUNK

    def band(t):
        tp = jnp.pad(t, ((0, 0), (WINDOW, 0), (0, 0), (0, 0)))
        return jnp.concatenate(
            [tp[:, o:o + S].reshape(B, nc, CHUNK, SWA_KV_HEADS, SWA_HD)
             for o in range(0, WINDOW + CHUNK, CHUNK)], axis=2)

    kpos = jnp.arange(nc)[:, None] * CHUNK - WINDOW + jnp.arange(WINDOW + CHUNK)[None, :]
    qb = q.reshape(B, nc, CHUNK, SWA_KV_HEADS, SWA_GROUP, SWA_HD)
    o = swa_core(qb, band(k), band(v), bias, kpos >= 0, sink)
    return o.reshape(B, S, SWA_W)


def swa_sample(q, k_new, v_new, k_buf, v_buf, bias, sink):
    B, T = q.shape[:2]
    k = jnp.concatenate([k_buf, k_new], axis=1)
    v = jnp.concatenate([v_buf, v_new], axis=1)
    qb = q.reshape(B, 1, T, SWA_KV_HEADS, SWA_GROUP, SWA_HD)
    valid = jnp.ones((1, WINDOW + T), dtype=bool)
    o = swa_core(qb, k[:, None], v[:, None], bias, valid, sink)
    return o.reshape(B, T, SWA_W), k[:, T:], v[:, T:]


def project_mem(mem, g, w):
    B = mem.shape[0]
    mk, mv = split_cols(rmsnorm(mem, g) @ w, (MEM_W, MEM_W))
    return mk.reshape(B, MEM_LEN, MEM_HEADS, MEM_HD), mv.reshape(B, MEM_LEN, MEM_HEADS, MEM_HD)


def mem_attend(qm, mk, mv):
    B, S = qm.shape[:2]
    q = qm.reshape(B, S, MEM_HEADS, MEM_HD)
    s = jnp.einsum('bqhd,bkhd->bhqk', q, mk).astype(jnp.float32) * MEM_HD ** -0.5
    p = jax.nn.softmax(s, axis=-1)
    return jnp.einsum('bhqk,bkhd->bqhd', p.astype(mv.dtype), mv).reshape(B, S, MEM_W)


def gate_out(o, om, z, w):
    u = jnp.concatenate([o, om], axis=-1) * jax.nn.silu(z)
    return u @ w


def setup_inputs(seed: int = 0) -> dict:
    key = jax.random.key(seed)
    ks = jax.random.split(key, 20)
    f32 = jnp.float32
    nrm = lambda k, shp, s=1.0: (jax.random.normal(k, shp, f32) * s)
    return {
        'x_prompt': nrm(ks[0], (BATCH, SEQ, D_MODEL)),
        'x_sample': nrm(ks[1], (DEC_BATCH, DEC_SEQ, D_MODEL)),
        'mem_prompt': nrm(ks[2], (BATCH, MEM_LEN, D_MODEL)),
        'cache_sb_k': nrm(ks[3], (N_A, DEC_BATCH, PAST_LEN, SB_HEADS, SB_HD)),
        'cache_sb_v': nrm(ks[4], (N_A, DEC_BATCH, PAST_LEN, SB_HEADS, SB_HD)),
        'cache_swa_k': nrm(ks[5], (N_B, DEC_BATCH, WINDOW, SWA_KV_HEADS, SWA_HD)),
        'cache_swa_v': nrm(ks[6], (N_B, DEC_BATCH, WINDOW, SWA_KV_HEADS, SWA_HD)),
        'cache_mem_k': nrm(ks[7], (DEPTH, DEC_BATCH, MEM_LEN, MEM_HEADS, MEM_HD)),
        'cache_mem_v': nrm(ks[8], (DEPTH, DEC_BATCH, MEM_LEN, MEM_HEADS, MEM_HD)),
        'pre_norm': 1.0 + nrm(ks[9], (DEPTH, D_MODEL), 0.05),
        'post_norm': 1.0 + nrm(ks[10], (DEPTH, D_MODEL), 0.05),
        'mem_norm': 1.0 + nrm(ks[11], (DEPTH, D_MODEL), 0.05),
        'w_in_a': nrm(ks[12], (N_A, D_MODEL, IN_A), D_MODEL ** -0.5),
        'w_in_b': nrm(ks[13], (N_B, D_MODEL, IN_B), D_MODEL ** -0.5),
        'w_mem_kv': nrm(ks[14], (DEPTH, D_MODEL, 2 * MEM_W), D_MODEL ** -0.5),
        'w_out': nrm(ks[15], (DEPTH, D_INNER, D_MODEL), D_INNER ** -0.5),
        'rel_bias': nrm(ks[16], (NUM_BUCKETS, SWA_HEADS), 0.5),
        'sinks': nrm(ks[17], (N_B, SWA_HEADS), 0.5),
    }


def reference(x_prompt, x_sample, mem_prompt,
              cache_sb_k, cache_sb_v, cache_swa_k, cache_swa_v, cache_mem_k, cache_mem_v,
              pre_norm, post_norm, mem_norm, w_in_a, w_in_b, w_mem_kv, w_out, rel_bias, sinks):
    B, S = x_prompt.shape[:2]
    Bd, T = x_sample.shape[:2]
    xp, xs = x_prompt, x_sample
    sb_kp, sb_vp, sb_ks, sb_vs = [], [], [], []
    swa_kp, swa_vp, swa_ks, swa_vs = [], [], [], []
    mem_kp, mem_vp = [], []
    bias_p = band_bias(rel_bias, CHUNK, WINDOW + CHUNK)
    bias_s = band_bias(rel_bias, T, WINDOW + T)
    for i in range(DEPTH):
        hp = rmsnorm(xp, pre_norm[i])
        hs = rmsnorm(xs, pre_norm[i])
        mk_p, mv_p = project_mem(mem_prompt, mem_norm[i], w_mem_kv[i])
        mem_kp.append(mk_p)
        mem_vp.append(mv_p)
        j = i // N_MIXERS
        if i % N_MIXERS == 0:
            widths = (SB_W, SB_W, SB_W, MEM_W, D_INNER)
            q_p, k_p, v_p, qm_p, z_p = split_cols(hp @ w_in_a[j], widths)
            q_s, k_s, v_s, qm_s, z_s = split_cols(hs @ w_in_a[j], widths)
            hshape_p = (B, S, SB_HEADS, SB_HD)
            hshape_s = (Bd, T, SB_HEADS, SB_HD)
            q_p, k_p, v_p = q_p.reshape(hshape_p), k_p.reshape(hshape_p), v_p.reshape(hshape_p)
            q_s, k_s, v_s = q_s.reshape(hshape_s), k_s.reshape(hshape_s), v_s.reshape(hshape_s)
            o_p = sb_prompt(q_p, k_p, v_p)
            o_s = sb_sample(q_s, k_s, v_s, cache_sb_k[j], cache_sb_v[j])
            sb_kp.append(k_p)
            sb_vp.append(v_p)
            sb_ks.append(k_s)
            sb_vs.append(v_s)
        else:
            widths = (SWA_W, SWA_KV_W, SWA_KV_W, MEM_W, D_INNER)
            q_p, k_p, v_p, qm_p, z_p = split_cols(hp @ w_in_b[j], widths)
            q_s, k_s, v_s, qm_s, z_s = split_cols(hs @ w_in_b[j], widths)
            q_p = q_p.reshape(B, S, SWA_HEADS, SWA_HD)
            k_p = k_p.reshape(B, S, SWA_KV_HEADS, SWA_HD)
            v_p = v_p.reshape(B, S, SWA_KV_HEADS, SWA_HD)
            q_s = q_s.reshape(Bd, T, SWA_HEADS, SWA_HD)
            k_s = k_s.reshape(Bd, T, SWA_KV_HEADS, SWA_HD)
            v_s = v_s.reshape(Bd, T, SWA_KV_HEADS, SWA_HD)
            o_p = swa_prompt(q_p, k_p, v_p, bias_p, sinks[j])
            o_s, kb_s, vb_s = swa_sample(q_s, k_s, v_s, cache_swa_k[j], cache_swa_v[j], bias_s, sinks[j])
            swa_kp.append(k_p[:, S - WINDOW:])
            swa_vp.append(v_p[:, S - WINDOW:])
            swa_ks.append(kb_s)
            swa_vs.append(vb_s)
        om_p = mem_attend(qm_p, mk_p, mv_p)
        om_s = mem_attend(qm_s, cache_mem_k[i], cache_mem_v[i])
        xp = xp + rmsnorm(gate_out(o_p, om_p, z_p, w_out[i]), post_norm[i])
        xs = xs + rmsnorm(gate_out(o_s, om_s, z_s, w_out[i]), post_norm[i])
    return (xp, xs,
            jnp.stack(sb_kp), jnp.stack(sb_vp), jnp.stack(sb_ks), jnp.stack(sb_vs),
            jnp.stack(swa_kp), jnp.stack(swa_vp), jnp.stack(swa_ks), jnp.stack(swa_vs),
            jnp.stack(mem_kp), jnp.stack(mem_vp))
```

```python
import functools
import math

import numpy as np
import jax
import jax.numpy as jnp
from jax import lax
from jax.experimental import pallas as pl
from jax.experimental.pallas import tpu as pltpu

F32 = jnp.float32
BF16 = jnp.bfloat16

D_MODEL = 1024
CHUNK = 64
MEM_LEN = 256
MEM_HEADS = 4
MEM_HD = 128
MEM_W = MEM_HEADS * MEM_HD
SB_HEADS = 12
SB_HD = 128
SB_W = SB_HEADS * SB_HD
SWA_HEADS = 24
SWA_KV_HEADS = 3
SWA_GROUP = SWA_HEADS // SWA_KV_HEADS
SWA_HD = 64
SWA_W = SWA_HEADS * SWA_HD
SWA_KV_W = SWA_KV_HEADS * SWA_HD
WINDOW = 128
BAND = WINDOW + CHUNK
NUM_BUCKETS = 32
MAX_DISTANCE = 128
EPS = 1e-6
NEG_INF = -1e30

VMEM_LIMIT_BYTES = 56 * 1024 * 1024


def _params(semantics):
    return pltpu.CompilerParams(dimension_semantics=semantics,
                                vmem_limit_bytes=VMEM_LIMIT_BYTES)


def _resident(shape):
    nd = len(shape)
    return pl.BlockSpec(shape, lambda *_: (0,) * nd, pipeline_mode=pl.Buffered(1))


def _rmsnorm(x, g):
    return x * lax.rsqrt(jnp.mean(x * x, axis=-1, keepdims=True) + EPS) * g


def _dot(a, b):
    return jnp.dot(a, b, preferred_element_type=F32)


def _dot_nt(a, b):
    return lax.dot_general(a, b, (((1,), (1,)), ((), ())), preferred_element_type=F32)


PROJ_COL_CHUNK = 512


def _norm_proj_kernel(x_ref, g_ref, *refs, widths):
    n = len(widths)
    w_refs, o_refs = refs[:n], refs[n:]
    h = _rmsnorm(x_ref[...], g_ref[...]).astype(BF16)
    for w_ref, o_ref, width in zip(w_refs, o_refs, widths):
        for c in range(0, width, PROJ_COL_CHUNK):
            cw = min(PROJ_COL_CHUNK, width - c)
            o_ref[:, c:c + cw] = _dot(h, w_ref[:, c:c + cw]).astype(o_ref.dtype)


def _norm_proj(x, gain, weights, dtypes, tm):
    n_tok, d = x.shape
    widths = tuple(w.shape[1] for w in weights)
    return pl.pallas_call(
        functools.partial(_norm_proj_kernel, widths=widths),
        grid=(n_tok // tm,),
        in_specs=[pl.BlockSpec((tm, d), lambda i: (i, 0)), _resident((1, d))]
        + [_resident(w.shape) for w in weights],
        out_specs=[pl.BlockSpec((tm, w), lambda i: (i, 0)) for w in widths],
        out_shape=[jax.ShapeDtypeStruct((n_tok, w), dt) for w, dt in zip(widths, dtypes)],
        compiler_params=_params(("parallel",)),
        name="norm_proj",
    )(x, gain.reshape(1, d), *weights)


def _mem_proj_kernel(x_ref, g_ref, w_ref, k_ref, v_ref):
    h = _rmsnorm(x_ref[...], g_ref[0]).astype(BF16)
    k_ref[0] = _dot(h, w_ref[0, :, :MEM_W])
    v_ref[0] = _dot(h, w_ref[0, :, MEM_W:])


def _mem_proj(mem, gains, w, tm):
    n_tok, d = mem.shape
    depth = w.shape[0]
    out = jax.ShapeDtypeStruct((depth, n_tok, MEM_W), F32)
    return pl.pallas_call(
        _mem_proj_kernel,
        grid=(depth, n_tok // tm),
        in_specs=[pl.BlockSpec((tm, d), lambda l, i: (i, 0)),
                  pl.BlockSpec((1, 1, d), lambda l, i: (l, 0, 0)),
                  pl.BlockSpec((1, d, 2 * MEM_W), lambda l, i: (l, 0, 0))],
        out_specs=[pl.BlockSpec((1, tm, MEM_W), lambda l, i: (l, i, 0))] * 2,
        out_shape=[out, out],
        compiler_params=_params(("parallel", "parallel")),
        name="mem_proj",
    )(mem, gains.reshape(depth, 1, d), w)


SB_SCALE = SB_HD ** -0.5
SB_TQ = 256
SB_TK = 256


def _later_key_matrix(n):
    later = lax.broadcasted_iota(jnp.int32, (n, n), 0) > lax.broadcasted_iota(jnp.int32, (n, n), 1)
    return jnp.where(later, 1.0, 0.0).astype(BF16)


def _sb_block(q, k, v, later, carry, acc, causal):
    z = _dot_nt(q, k) * SB_SCALE
    log_1m = -(jnp.maximum(z, 0.0) + jnp.log(1.0 + jnp.exp(-jnp.abs(z))))
    if causal is not None:
        log_1m = jnp.where(causal, log_1m, 0.0)
    hi = log_1m.astype(BF16)
    lo = (log_1m - hi.astype(F32)).astype(BF16)
    suffix = _dot(hi, later) + _dot(lo, later) + carry
    a = jnp.exp(z + log_1m + suffix)
    if causal is not None:
        a = jnp.where(causal, a, 0.0)
    acc = acc + _dot(a.astype(BF16), v)
    carry = carry + jnp.sum(log_1m, axis=-1, keepdims=True)
    return carry, acc


def _strict_causal(n):
    return lax.broadcasted_iota(jnp.int32, (n, n), 1) < lax.broadcasted_iota(jnp.int32, (n, n), 0)


def _sb_prompt_kernel(q_ref, k_ref, v_ref, o_ref, kb_ref, vb_ref):
    i = pl.program_id(2)

    @pl.when(i == 0)
    def _():
        kb_ref[...] = k_ref[...].astype(BF16)
        vb_ref[...] = v_ref[...].astype(BF16)

    q = q_ref[...]
    later = _later_key_matrix(SB_TK)
    start = pl.multiple_of(i * SB_TK, SB_TK)
    carry = jnp.zeros((SB_TQ, 1), F32)
    acc = jnp.zeros((SB_TQ, SB_HD), F32)
    carry, acc = _sb_block(q, kb_ref[pl.ds(start, SB_TK), :], vb_ref[pl.ds(start, SB_TK), :],
                           later, carry, acc, _strict_causal(SB_TQ))

    def body(jj, state):
        s = pl.multiple_of((i - 1 - jj) * SB_TK, SB_TK)
        return _sb_block(q, kb_ref[pl.ds(s, SB_TK), :], vb_ref[pl.ds(s, SB_TK), :],
                         later, state[0], state[1], None)

    carry, acc = lax.fori_loop(0, i, body, (carry, acc))
    o_ref[...] = acc


def _sb_prompt(q, k, v, batch, seq):
    n_tok = batch * seq
    nq = seq // SB_TQ
    return pl.pallas_call(
        _sb_prompt_kernel,
        grid=(batch, SB_HEADS, nq),
        in_specs=[pl.BlockSpec((SB_TQ, SB_HD), lambda b, h, i: (b * nq + i, h)),
                  pl.BlockSpec((seq, SB_HD), lambda b, h, i: (b, h)),
                  pl.BlockSpec((seq, SB_HD), lambda b, h, i: (b, h))],
        out_specs=pl.BlockSpec((SB_TQ, SB_HD), lambda b, h, i: (b * nq + i, h)),
        out_shape=jax.ShapeDtypeStruct((n_tok, SB_W), F32),
        scratch_shapes=[pltpu.VMEM((seq, SB_HD), BF16), pltpu.VMEM((seq, SB_HD), BF16)],
        compiler_params=_params(("parallel", "parallel", "arbitrary")),
        name="sb_prompt",
    )(q, k, v)


def _sb_sample_kernel(q_ref, kn_ref, vn_ref, kc_ref, vc_ref, o_ref, *, t_new, past):
    q = q_ref[...]
    carry = jnp.zeros((t_new, 1), F32)
    acc = jnp.zeros((t_new, SB_HD), F32)
    carry, acc = _sb_block(q, kn_ref[...].astype(BF16), vn_ref[...].astype(BF16),
                           _later_key_matrix(t_new), carry, acc, _strict_causal(t_new))
    later = _later_key_matrix(SB_TK)
    nblk = past // SB_TK

    def body(jj, state):
        s = pl.multiple_of((nblk - 1 - jj) * SB_TK, SB_TK)
        return _sb_block(q, kc_ref[pl.ds(s, SB_TK), :].astype(BF16),
                         vc_ref[pl.ds(s, SB_TK), :].astype(BF16), later, state[0], state[1], None)

    carry, acc = lax.fori_loop(0, nblk, body, (carry, acc))
    o_ref[...] = acc


def _sb_sample(q, k_new, v_new, k_cache, v_cache, batch, t_new, past):
    new_spec = pl.BlockSpec((t_new, SB_HD), lambda b, h: (b, h))
    cache_spec = pl.BlockSpec((past, SB_HD), lambda b, h: (b, h))
    return pl.pallas_call(
        functools.partial(_sb_sample_kernel, t_new=t_new, past=past),
        grid=(batch, SB_HEADS),
        in_specs=[new_spec, new_spec, new_spec, cache_spec, cache_spec],
        out_specs=new_spec,
        out_shape=jax.ShapeDtypeStruct((batch * t_new, SB_W), F32),
        compiler_params=_params(("parallel", "parallel")),
        name="sb_sample",
    )(q, k_new, v_new, k_cache, v_cache)


SWA_SCALE = SWA_HD ** -0.5
SWA_ROWS = SWA_GROUP * CHUNK


def _swa_kernel(q_ref, k_ref, v_ref, kf_ref, vf_ref, bias_ref, sink_ref, o_ref, kp_ref, vp_ref,
                *, chunks, front_valid):
    t = pl.program_id(1)

    @pl.when(t == 0)
    def _():
        kp_ref[:WINDOW, :] = kf_ref[...].astype(BF16)
        vp_ref[:WINDOW, :] = vf_ref[...].astype(BF16)
        kp_ref[WINDOW:, :] = k_ref[...].astype(BF16)
        vp_ref[WINDOW:, :] = v_ref[...].astype(BF16)

    for c in range(chunks):
        cg = t * chunks + c
        start = pl.multiple_of(cg * CHUNK, CHUNK)
        kband = kp_ref[pl.ds(start, BAND), :]
        vband = vp_ref[pl.ds(start, BAND), :]
        if not front_valid:
            key_pos = lax.broadcasted_iota(jnp.int32, (1, BAND), 1) + (cg * CHUNK - WINDOW)
            valid = key_pos >= 0
        for g in range(SWA_KV_HEADS):
            kg = kband[:, g * SWA_HD:(g + 1) * SWA_HD]
            vg = vband[:, g * SWA_HD:(g + 1) * SWA_HD]
            qs = jnp.concatenate(
                [q_ref[c * CHUNK:(c + 1) * CHUNK, (g * SWA_GROUP + r) * SWA_HD:(g * SWA_GROUP + r + 1) * SWA_HD]
                 for r in range(SWA_GROUP)], axis=0)
            s = _dot_nt(qs, kg) * SWA_SCALE + bias_ref[g]
            if not front_valid:
                s = jnp.where(valid, s, NEG_INF)
            sk = sink_ref[g]
            m = jnp.maximum(jnp.max(s, axis=-1, keepdims=True), sk)
            p = jnp.exp(s - m)
            den = jnp.sum(p, axis=-1, keepdims=True) + jnp.exp(sk - m)
            p = p / den
            og = _dot(p.astype(BF16), vg)
            for r in range(SWA_GROUP):
                h = g * SWA_GROUP + r
                o_ref[c * CHUNK:(c + 1) * CHUNK, h * SWA_HD:(h + 1) * SWA_HD] = og[r * CHUNK:(r + 1) * CHUNK]


def _swa(q, k, v, k_front, v_front, bias, sink, batch, seq, front_valid):
    chunks = min(4, seq // CHUNK)
    tile = chunks * CHUNK
    nt = seq // tile
    kv_spec = pl.BlockSpec((seq, SWA_KV_W), lambda b, t: (b, 0))
    front_spec = pl.BlockSpec((WINDOW, SWA_KV_W), lambda b, t: (b, 0))
    return pl.pallas_call(
        functools.partial(_swa_kernel, chunks=chunks, front_valid=front_valid),
        grid=(batch, nt),
        in_specs=[pl.BlockSpec((tile, SWA_W), lambda b, t: (b * nt + t, 0)),
                  kv_spec, kv_spec, front_spec, front_spec,
                  _resident(bias.shape), _resident(sink.shape)],
        out_specs=pl.BlockSpec((tile, SWA_W), lambda b, t: (b * nt + t, 0)),
        out_shape=jax.ShapeDtypeStruct((batch * seq, SWA_W), F32),
        scratch_shapes=[pltpu.VMEM((seq + WINDOW, SWA_KV_W), BF16)] * 2,
        compiler_params=_params(("parallel", "arbitrary")),
        name="swa",
    )(q, k, v, k_front, v_front, bias, sink)


def _t5_bucket(rel):
    nb = NUM_BUCKETS // 2
    max_exact = nb // 2
    ret = np.where(rel > 0, nb, 0)
    n = np.abs(rel)
    nf = np.maximum(n, 1).astype(np.float32)
    large = max_exact + (np.log(nf / np.float32(max_exact)) / np.float32(math.log(MAX_DISTANCE / max_exact))
                         * np.float32(nb - max_exact)).astype(np.int32)
    large = np.minimum(large, nb - 1)
    return ret + np.where(n < max_exact, n, large)


def _band_bias(rel_bias):
    rel = np.arange(BAND)[None, :] - WINDOW - np.arange(CHUNK)[:, None]
    b = jnp.take(rel_bias, jnp.asarray(_t5_bucket(rel), jnp.int32), axis=0)
    return jnp.transpose(b, (2, 0, 1)).reshape(SWA_KV_HEADS, SWA_ROWS, BAND).astype(F32)


def _sink_rows(sink):
    return jnp.repeat(sink.astype(F32), CHUNK).reshape(SWA_KV_HEADS, SWA_ROWS, 1)


MEM_SCALE = MEM_HD ** -0.5


def _mem_attn_kernel(q_ref, k_ref, v_ref, o_ref):
    for h in range(MEM_HEADS):
        cols = slice(h * MEM_HD, (h + 1) * MEM_HD)
        s = _dot_nt(q_ref[:, cols], k_ref[:, cols].astype(BF16)) * MEM_SCALE
        p = jnp.exp(s - jnp.max(s, axis=-1, keepdims=True))
        p = p / jnp.sum(p, axis=-1, keepdims=True)
        o_ref[:, cols] = _dot(p.astype(BF16), v_ref[:, cols].astype(BF16))


def _mem_attn(qm, mk, mv, batch, seq, tq):
    nt = seq // tq
    kv_spec = pl.BlockSpec((MEM_LEN, MEM_W), lambda b, t: (b, 0))
    return pl.pallas_call(
        _mem_attn_kernel,
        grid=(batch, nt),
        in_specs=[pl.BlockSpec((tq, MEM_W), lambda b, t: (b * nt + t, 0)), kv_spec, kv_spec],
        out_specs=pl.BlockSpec((tq, MEM_W), lambda b, t: (b * nt + t, 0)),
        out_shape=jax.ShapeDtypeStruct((batch * seq, MEM_W), F32),
        compiler_params=_params(("parallel", "parallel")),
        name="mem_attn",
    )(qm, mk, mv)


def _silu(z):
    return z * (1.0 / (1.0 + jnp.exp(-z)))


def _out_proj_kernel(o_ref, om_ref, z_ref, x_ref, w_ref, g_ref, y_ref, *, mix_w):
    u1 = (o_ref[...] * _silu(z_ref[:, :mix_w])).astype(BF16)
    u2 = (om_ref[...] * _silu(z_ref[:, mix_w:])).astype(BF16)
    y = _dot(u1, w_ref[:mix_w, :]) + _dot(u2, w_ref[mix_w:, :])
    y_ref[...] = x_ref[...] + _rmsnorm(y, g_ref[...])


def _out_proj(o, om, z, x, w, gain, tm):
    n_tok, d = x.shape
    mix_w = o.shape[1]
    row = lambda width: pl.BlockSpec((tm, width), lambda i: (i, 0))
    return pl.pallas_call(
        functools.partial(_out_proj_kernel, mix_w=mix_w),
        grid=(n_tok // tm,),
        in_specs=[row(mix_w), row(MEM_W), row(mix_w + MEM_W), row(d),
                  _resident(w.shape), _resident((1, d))],
        out_specs=row(d),
        out_shape=jax.ShapeDtypeStruct((n_tok, d), F32),
        compiler_params=_params(("parallel",)),
        name="out_proj",
    )(o, om, z, x, w, gain.reshape(1, d))


def _split_cols(w, widths):
    out, off = [], 0
    for width in widths:
        out.append(w[:, off:off + width].astype(BF16))
        off += width
    return out


@jax.jit
def kernel(x_prompt, x_sample, mem_prompt, cache_sb_k, cache_sb_v, cache_swa_k, cache_swa_v,
           cache_mem_k, cache_mem_v, pre_norm, post_norm, mem_norm, w_in_a, w_in_b, w_mem_kv,
           w_out, rel_bias, sinks):
    batch, seq, d = x_prompt.shape
    dec_batch, t_new, _ = x_sample.shape
    depth = pre_norm.shape[0]
    past = cache_sb_k.shape[2]
    n_p, n_s = batch * seq, dec_batch * t_new
    d_inner = w_out.shape[1]

    xp = x_prompt.reshape(n_p, d)
    xs = x_sample.reshape(n_s, d)
    proj_dtypes = (BF16, F32, F32, BF16, F32)
    widths_a = (SB_W, SB_W, SB_W, MEM_W, d_inner)
    widths_b = (SWA_W, SWA_KV_W, SWA_KV_W, MEM_W, d_inner)

    mem_k, mem_v = _mem_proj(mem_prompt.reshape(batch * MEM_LEN, d), mem_norm,
                             w_mem_kv.astype(BF16), tm=512)
    bias = _band_bias(rel_bias)
    zero_front = jnp.zeros((batch * WINDOW, SWA_KV_W), F32)

    sb_kp, sb_vp, sb_ks, sb_vs = [], [], [], []
    swa_kp, swa_vp, swa_ks, swa_vs = [], [], [], []
    for i in range(depth):
        j = i // 2
        if i % 2 == 0:
            weights = _split_cols(w_in_a[j], widths_a)
            q_p, k_p, v_p, qm_p, z_p = _norm_proj(xp, pre_norm[i], weights, proj_dtypes, tm=512)
            q_s, k_s, v_s, qm_s, z_s = _norm_proj(xs, pre_norm[i], weights, proj_dtypes, tm=256)
            o_p = _sb_prompt(q_p, k_p, v_p, batch, seq)
            o_s = _sb_sample(q_s, k_s, v_s, cache_sb_k[j].reshape(dec_batch * past, SB_W),
                             cache_sb_v[j].reshape(dec_batch * past, SB_W), dec_batch, t_new, past)
            sb_kp.append(k_p)
            sb_vp.append(v_p)
            sb_ks.append(k_s)
            sb_vs.append(v_s)
        else:
            weights = _split_cols(w_in_b[j], widths_b)
            q_p, k_p, v_p, qm_p, z_p = _norm_proj(xp, pre_norm[i], weights, proj_dtypes, tm=512)
            q_s, k_s, v_s, qm_s, z_s = _norm_proj(xs, pre_norm[i], weights, proj_dtypes, tm=256)
            sink = _sink_rows(sinks[j])
            o_p = _swa(q_p, k_p, v_p, zero_front, zero_front, bias, sink, batch, seq, front_valid=False)
            kc = cache_swa_k[j].reshape(dec_batch * WINDOW, SWA_KV_W)
            vc = cache_swa_v[j].reshape(dec_batch * WINDOW, SWA_KV_W)
            o_s = _swa(q_s, k_s, v_s, kc, vc, bias, sink, dec_batch, t_new, front_valid=True)
            swa_kp.append(k_p.reshape(batch, seq, SWA_KV_W)[:, seq - WINDOW:])
            swa_vp.append(v_p.reshape(batch, seq, SWA_KV_W)[:, seq - WINDOW:])
            k_all = jnp.concatenate([kc.reshape(dec_batch, WINDOW, SWA_KV_W),
                                     k_s.reshape(dec_batch, t_new, SWA_KV_W)], axis=1)
            v_all = jnp.concatenate([vc.reshape(dec_batch, WINDOW, SWA_KV_W),
                                     v_s.reshape(dec_batch, t_new, SWA_KV_W)], axis=1)
            swa_ks.append(k_all[:, t_new:])
            swa_vs.append(v_all[:, t_new:])
        om_p = _mem_attn(qm_p, mem_k[i], mem_v[i], batch, seq, tq=512)
        om_s = _mem_attn(qm_s, cache_mem_k[i].reshape(dec_batch * MEM_LEN, MEM_W),
                         cache_mem_v[i].reshape(dec_batch * MEM_LEN, MEM_W), dec_batch, t_new, tq=t_new)
        w_o = w_out[i].astype(BF16)
        xp = _out_proj(o_p, om_p, z_p, xp, w_o, post_norm[i], tm=512)
        xs = _out_proj(o_s, om_s, z_s, xs, w_o, post_norm[i], tm=256)

    n_a, n_b = len(sb_kp), len(swa_kp)
    return (xp.reshape(batch, seq, d), xs.reshape(dec_batch, t_new, d),
            jnp.stack(sb_kp).reshape(n_a, batch, seq, SB_HEADS, SB_HD),
            jnp.stack(sb_vp).reshape(n_a, batch, seq, SB_HEADS, SB_HD),
            jnp.stack(sb_ks).reshape(n_a, dec_batch, t_new, SB_HEADS, SB_HD),
            jnp.stack(sb_vs).reshape(n_a, dec_batch, t_new, SB_HEADS, SB_HD),
            jnp.stack(swa_kp).reshape(n_b, batch, WINDOW, SWA_KV_HEADS, SWA_HD),
            jnp.stack(swa_vp).reshape(n_b, batch, WINDOW, SWA_KV_HEADS, SWA_HD),
            jnp.stack(swa_ks).reshape(n_b, dec_batch, WINDOW, SWA_KV_HEADS, SWA_HD),
            jnp.stack(swa_vs).reshape(n_b, dec_batch, WINDOW, SWA_KV_HEADS, SWA_HD),
            mem_k.reshape(depth, batch, MEM_LEN, MEM_HEADS, MEM_HD),
            mem_v.reshape(depth, batch, MEM_LEN, MEM_HEADS, MEM_HD))
```

```python
import functools
import math

import numpy as np
import jax
import jax.numpy as jnp
from jax import lax
from jax.experimental import pallas as pl
from jax.experimental.pallas import tpu as pltpu

F32 = jnp.float32
BF16 = jnp.bfloat16

D_MODEL = 1024
CHUNK = 64
MEM_LEN = 256
MEM_HEADS = 4
MEM_HD = 128
MEM_W = MEM_HEADS * MEM_HD
SB_HEADS = 12
SB_HD = 128
SB_W = SB_HEADS * SB_HD
SWA_HEADS = 24
SWA_KV_HEADS = 3
SWA_GROUP = SWA_HEADS // SWA_KV_HEADS
SWA_HD = 64
SWA_W = SWA_HEADS * SWA_HD
SWA_KV_W = SWA_KV_HEADS * SWA_HD
WINDOW = 128
BAND = WINDOW + CHUNK
NUM_BUCKETS = 32
MAX_DISTANCE = 128
EPS = 1e-6
NEG_INF = -1e30

VMEM_LIMIT_BYTES = 56 * 1024 * 1024


def _params(semantics):
    return pltpu.CompilerParams(dimension_semantics=semantics,
                                vmem_limit_bytes=VMEM_LIMIT_BYTES)


def _resident(shape):
    nd = len(shape)
    return pl.BlockSpec(shape, lambda *_: (0,) * nd, pipeline_mode=pl.Buffered(1))


def _rmsnorm(x, g):
    return x * lax.rsqrt(jnp.mean(x * x, axis=-1, keepdims=True) + EPS) * g


def _dot(a, b):
    return jnp.dot(a, b, preferred_element_type=F32)


def _dot_nt(a, b):
    return lax.dot_general(a, b, (((1,), (1,)), ((), ())), preferred_element_type=F32)


PROJ_COL_CHUNK = 512


def _norm_proj_kernel(x_ref, g_ref, *refs, widths):
    n = len(widths)
    w_refs, o_refs = refs[:n], refs[n:]
    h = _rmsnorm(x_ref[...], g_ref[...]).astype(BF16)
    for w_ref, o_ref, width in zip(w_refs, o_refs, widths):
        for c in range(0, width, PROJ_COL_CHUNK):
            cw = min(PROJ_COL_CHUNK, width - c)
            o_ref[:, c:c + cw] = _dot(h, w_ref[:, c:c + cw]).astype(o_ref.dtype)


def _norm_proj(x, gain, weights, dtypes, tm):
    n_tok, d = x.shape
    widths = tuple(w.shape[1] for w in weights)
    return pl.pallas_call(
        functools.partial(_norm_proj_kernel, widths=widths),
        grid=(n_tok // tm,),
        in_specs=[pl.BlockSpec((tm, d), lambda i: (i, 0)), _resident((1, d))]
        + [_resident(w.shape) for w in weights],
        out_specs=[pl.BlockSpec((tm, w), lambda i: (i, 0)) for w in widths],
        out_shape=[jax.ShapeDtypeStruct((n_tok, w), dt) for w, dt in zip(widths, dtypes)],
        compiler_params=_params(("parallel",)),
        name="norm_proj",
    )(x, gain.reshape(1, d), *weights)


def _mem_proj_kernel(x_ref, g_ref, w_ref, k_ref, v_ref):
    h = _rmsnorm(x_ref[...], g_ref[0]).astype(BF16)
    k_ref[0] = _dot(h, w_ref[0, :, :MEM_W])
    v_ref[0] = _dot(h, w_ref[0, :, MEM_W:])


def _mem_proj(mem, gains, w, tm):
    n_tok, d = mem.shape
    depth = w.shape[0]
    out = jax.ShapeDtypeStruct((depth, n_tok, MEM_W), F32)
    return pl.pallas_call(
        _mem_proj_kernel,
        grid=(depth, n_tok // tm),
        in_specs=[pl.BlockSpec((tm, d), lambda l, i: (i, 0)),
                  pl.BlockSpec((1, 1, d), lambda l, i: (l, 0, 0)),
                  pl.BlockSpec((1, d, 2 * MEM_W), lambda l, i: (l, 0, 0))],
        out_specs=[pl.BlockSpec((1, tm, MEM_W), lambda l, i: (l, i, 0))] * 2,
        out_shape=[out, out],
        compiler_params=_params(("parallel", "parallel")),
        name="mem_proj",
    )(mem, gains.reshape(depth, 1, d), w)


LOG2E = math.log2(math.e)
SB_LOG2_SCALE = SB_HD ** -0.5 * LOG2E
SB_TQ = 256
SB_TK = 256
SB_HEADS_PER_STEP = 6
SB_SAMPLE_HEADS_PER_STEP = 3
SB_DEAD_LOG2 = 151.0


def _later_key_matrix(n):
    later = lax.broadcasted_iota(jnp.int32, (n, n), 0) > lax.broadcasted_iota(jnp.int32, (n, n), 1)
    return jnp.where(later, 1.0, 0.0).astype(BF16)


def _strict_causal(n):
    return lax.broadcasted_iota(jnp.int32, (n, n), 1) < lax.broadcasted_iota(jnp.int32, (n, n), 0)


def _sb_visit(heads, q_ref, load_k, load_v, later, state, causal):
    cols = [slice(c * SB_HD, (c + 1) * SB_HD) for c in range(heads)]
    z2 = [_dot_nt(q_ref[:, cs], load_k(cs)) * SB_LOG2_SCALE for cs in cols]
    nl = [jnp.maximum(z, 0.0) + jnp.log2(1.0 + jnp.exp2(-jnp.abs(z))) for z in z2]
    if causal is not None:
        nl = [jnp.where(causal, x, 0.0) for x in nl]
    hi = [x.astype(BF16) for x in nl]
    lo = [(x - h.astype(F32)).astype(BF16) for x, h in zip(nl, hi)]
    nsuffix = [_dot(h, later) + _dot(l, later) + st[0] for h, l, st in zip(hi, lo, state)]
    a = [jnp.exp2(z - x - s) for z, x, s in zip(z2, nl, nsuffix)]
    if causal is not None:
        a = [jnp.where(causal, x, 0.0) for x in a]
    acc = [st[1] + _dot(x.astype(BF16), load_v(cs)) for x, cs, st in zip(a, cols, state)]
    ncarry = [st[0] + jnp.sum(x, axis=-1, keepdims=True) for x, st in zip(nl, state)]
    return tuple(zip(ncarry, acc))


def _sb_older_blocks(n_blocks, visit, state):
    def cond(st):
        return jnp.logical_and(st[0] < n_blocks, st[1] > 0)

    def body(st):
        new = visit(st[0], st[2])
        least = functools.reduce(jnp.minimum, [s[0] for s in new])
        alive = (jnp.min(least) < SB_DEAD_LOG2).astype(jnp.int32)
        return st[0] + 1, alive, new

    return lax.while_loop(cond, body, (jnp.int32(0), jnp.int32(1), state))[2]


def _sb_init(heads, tq):
    return tuple((jnp.zeros((tq, 1), F32), jnp.zeros((tq, SB_HD), F32)) for _ in range(heads))


def _sb_prompt_kernel(q_ref, k_ref, v_ref, o_ref, kb_ref, vb_ref, *, heads):
    i = pl.program_id(2)

    @pl.when(i == 0)
    def _():
        kb_ref[...] = k_ref[...].astype(BF16)
        vb_ref[...] = v_ref[...].astype(BF16)

    later = _later_key_matrix(SB_TK)

    def visit_at(block, state, causal):
        rows = pl.ds(pl.multiple_of(block * SB_TK, SB_TK), SB_TK)
        return _sb_visit(heads, q_ref, lambda cols: kb_ref[rows, cols], lambda cols: vb_ref[rows, cols],
                         later, state, causal)

    state = visit_at(i, _sb_init(heads, SB_TQ), _strict_causal(SB_TQ))
    state = _sb_older_blocks(i, lambda jj, st: visit_at(i - 1 - jj, st, None), state)
    for c in range(heads):
        o_ref[:, c * SB_HD:(c + 1) * SB_HD] = state[c][1]


def _sb_prompt(q, k, v, batch, seq):
    n_tok = batch * seq
    nq = seq // SB_TQ
    heads = SB_HEADS_PER_STEP
    width = heads * SB_HD
    return pl.pallas_call(
        functools.partial(_sb_prompt_kernel, heads=heads),
        grid=(batch, SB_HEADS // heads, nq),
        in_specs=[pl.BlockSpec((SB_TQ, width), lambda b, h, i: (b * nq + i, h)),
                  pl.BlockSpec((seq, width), lambda b, h, i: (b, h)),
                  pl.BlockSpec((seq, width), lambda b, h, i: (b, h))],
        out_specs=pl.BlockSpec((SB_TQ, width), lambda b, h, i: (b * nq + i, h)),
        out_shape=jax.ShapeDtypeStruct((n_tok, SB_W), F32),
        scratch_shapes=[pltpu.VMEM((seq, width), BF16), pltpu.VMEM((seq, width), BF16)],
        compiler_params=_params(("parallel", "parallel", "arbitrary")),
        name="sb_prompt",
    )(q, k, v)


def _sb_sample_kernel(q_ref, kn_ref, vn_ref, kc_ref, vc_ref, o_ref, *, heads, t_new, past):
    state = _sb_visit(heads, q_ref, lambda cols: kn_ref[:, cols].astype(BF16),
                      lambda cols: vn_ref[:, cols].astype(BF16), _later_key_matrix(t_new),
                      _sb_init(heads, t_new), _strict_causal(t_new))
    later = _later_key_matrix(SB_TK)
    n_blocks = past // SB_TK

    def visit(jj, st):
        rows = pl.ds(pl.multiple_of((n_blocks - 1 - jj) * SB_TK, SB_TK), SB_TK)
        return _sb_visit(heads, q_ref, lambda cols: kc_ref[rows, cols].astype(BF16),
                         lambda cols: vc_ref[rows, cols].astype(BF16), later, st, None)

    state = _sb_older_blocks(n_blocks, visit, state)
    for c in range(heads):
        o_ref[:, c * SB_HD:(c + 1) * SB_HD] = state[c][1]


def _sb_sample(q, k_new, v_new, k_cache, v_cache, batch, t_new, past):
    heads = SB_SAMPLE_HEADS_PER_STEP
    width = heads * SB_HD
    new_spec = pl.BlockSpec((t_new, width), lambda b, h: (b, h))
    cache_spec = pl.BlockSpec((past, width), lambda b, h: (b, h))
    return pl.pallas_call(
        functools.partial(_sb_sample_kernel, heads=heads, t_new=t_new, past=past),
        grid=(batch, SB_HEADS // heads),
        in_specs=[new_spec, new_spec, new_spec, cache_spec, cache_spec],
        out_specs=new_spec,
        out_shape=jax.ShapeDtypeStruct((batch * t_new, SB_W), F32),
        compiler_params=_params(("parallel", "parallel")),
        name="sb_sample",
    )(q, k_new, v_new, k_cache, v_cache)


SWA_SCALE = SWA_HD ** -0.5
SWA_ROWS = SWA_GROUP * CHUNK


def _swa_kernel(q_ref, k_ref, v_ref, kf_ref, vf_ref, bias_ref, sink_ref, o_ref, kp_ref, vp_ref,
                *, chunks, front_valid):
    t = pl.program_id(1)

    @pl.when(t == 0)
    def _():
        kp_ref[:WINDOW, :] = kf_ref[...].astype(BF16)
        vp_ref[:WINDOW, :] = vf_ref[...].astype(BF16)
        kp_ref[WINDOW:, :] = k_ref[...].astype(BF16)
        vp_ref[WINDOW:, :] = v_ref[...].astype(BF16)

    for c in range(chunks):
        cg = t * chunks + c
        start = pl.multiple_of(cg * CHUNK, CHUNK)
        kband = kp_ref[pl.ds(start, BAND), :]
        vband = vp_ref[pl.ds(start, BAND), :]
        if not front_valid:
            key_pos = lax.broadcasted_iota(jnp.int32, (1, BAND), 1) + (cg * CHUNK - WINDOW)
            valid = key_pos >= 0
        for g in range(SWA_KV_HEADS):
            kg = kband[:, g * SWA_HD:(g + 1) * SWA_HD]
            vg = vband[:, g * SWA_HD:(g + 1) * SWA_HD]
            qs = jnp.concatenate(
                [q_ref[c * CHUNK:(c + 1) * CHUNK, (g * SWA_GROUP + r) * SWA_HD:(g * SWA_GROUP + r + 1) * SWA_HD]
                 for r in range(SWA_GROUP)], axis=0)
            s = _dot_nt(qs, kg) * SWA_SCALE + bias_ref[g]
            if not front_valid:
                s = jnp.where(valid, s, NEG_INF)
            sk = sink_ref[g]
            m = jnp.maximum(jnp.max(s, axis=-1, keepdims=True), sk)
            p = jnp.exp(s - m)
            den = jnp.sum(p, axis=-1, keepdims=True) + jnp.exp(sk - m)
            p = p / den
            og = _dot(p.astype(BF16), vg)
            for r in range(SWA_GROUP):
                h = g * SWA_GROUP + r
                o_ref[c * CHUNK:(c + 1) * CHUNK, h * SWA_HD:(h + 1) * SWA_HD] = og[r * CHUNK:(r + 1) * CHUNK]


def _swa(q, k, v, k_front, v_front, bias, sink, batch, seq, front_valid):
    chunks = min(4, seq // CHUNK)
    tile = chunks * CHUNK
    nt = seq // tile
    kv_spec = pl.BlockSpec((seq, SWA_KV_W), lambda b, t: (b, 0))
    front_spec = pl.BlockSpec((WINDOW, SWA_KV_W), lambda b, t: (b, 0))
    return pl.pallas_call(
        functools.partial(_swa_kernel, chunks=chunks, front_valid=front_valid),
        grid=(batch, nt),
        in_specs=[pl.BlockSpec((tile, SWA_W), lambda b, t: (b * nt + t, 0)),
                  kv_spec, kv_spec, front_spec, front_spec,
                  _resident(bias.shape), _resident(sink.shape)],
        out_specs=pl.BlockSpec((tile, SWA_W), lambda b, t: (b * nt + t, 0)),
        out_shape=jax.ShapeDtypeStruct((batch * seq, SWA_W), F32),
        scratch_shapes=[pltpu.VMEM((seq + WINDOW, SWA_KV_W), BF16)] * 2,
        compiler_params=_params(("parallel", "arbitrary")),
        name="swa",
    )(q, k, v, k_front, v_front, bias, sink)


def _t5_bucket(rel):
    nb = NUM_BUCKETS // 2
    max_exact = nb // 2
    ret = np.where(rel > 0, nb, 0)
    n = np.abs(rel)
    nf = np.maximum(n, 1).astype(np.float32)
    large = max_exact + (np.log(nf / np.float32(max_exact)) / np.float32(math.log(MAX_DISTANCE / max_exact))
                         * np.float32(nb - max_exact)).astype(np.int32)
    large = np.minimum(large, nb - 1)
    return ret + np.where(n < max_exact, n, large)


def _band_bias(rel_bias):
    rel = np.arange(BAND)[None, :] - WINDOW - np.arange(CHUNK)[:, None]
    b = jnp.take(rel_bias, jnp.asarray(_t5_bucket(rel), jnp.int32), axis=0)
    return jnp.transpose(b, (2, 0, 1)).reshape(SWA_KV_HEADS, SWA_ROWS, BAND).astype(F32)


def _sink_rows(sink):
    return jnp.repeat(sink.astype(F32), CHUNK).reshape(SWA_KV_HEADS, SWA_ROWS, 1)


MEM_SCALE = MEM_HD ** -0.5


def _mem_attn_kernel(q_ref, k_ref, v_ref, o_ref):
    for h in range(MEM_HEADS):
        cols = slice(h * MEM_HD, (h + 1) * MEM_HD)
        s = _dot_nt(q_ref[:, cols], k_ref[:, cols].astype(BF16)) * MEM_SCALE
        p = jnp.exp(s - jnp.max(s, axis=-1, keepdims=True))
        p = p / jnp.sum(p, axis=-1, keepdims=True)
        o_ref[:, cols] = _dot(p.astype(BF16), v_ref[:, cols].astype(BF16))


def _mem_attn(qm, mk, mv, batch, seq, tq):
    nt = seq // tq
    kv_spec = pl.BlockSpec((MEM_LEN, MEM_W), lambda b, t: (b, 0))
    return pl.pallas_call(
        _mem_attn_kernel,
        grid=(batch, nt),
        in_specs=[pl.BlockSpec((tq, MEM_W), lambda b, t: (b * nt + t, 0)), kv_spec, kv_spec],
        out_specs=pl.BlockSpec((tq, MEM_W), lambda b, t: (b * nt + t, 0)),
        out_shape=jax.ShapeDtypeStruct((batch * seq, MEM_W), F32),
        compiler_params=_params(("parallel", "parallel")),
        name="mem_attn",
    )(qm, mk, mv)


def _silu(z):
    return z * (1.0 / (1.0 + jnp.exp(-z)))


def _out_proj_kernel(o_ref, om_ref, z_ref, x_ref, w_ref, g_ref, y_ref, *, mix_w):
    u1 = (o_ref[...] * _silu(z_ref[:, :mix_w])).astype(BF16)
    u2 = (om_ref[...] * _silu(z_ref[:, mix_w:])).astype(BF16)
    y = _dot(u1, w_ref[:mix_w, :]) + _dot(u2, w_ref[mix_w:, :])
    y_ref[...] = x_ref[...] + _rmsnorm(y, g_ref[...])


def _out_proj(o, om, z, x, w, gain, tm):
    n_tok, d = x.shape
    mix_w = o.shape[1]
    row = lambda width: pl.BlockSpec((tm, width), lambda i: (i, 0))
    return pl.pallas_call(
        functools.partial(_out_proj_kernel, mix_w=mix_w),
        grid=(n_tok // tm,),
        in_specs=[row(mix_w), row(MEM_W), row(mix_w + MEM_W), row(d),
                  _resident(w.shape), _resident((1, d))],
        out_specs=row(d),
        out_shape=jax.ShapeDtypeStruct((n_tok, d), F32),
        compiler_params=_params(("parallel",)),
        name="out_proj",
    )(o, om, z, x, w, gain.reshape(1, d))


def _split_cols(w, widths):
    out, off = [], 0
    for width in widths:
        out.append(w[:, off:off + width].astype(BF16))
        off += width
    return out


@jax.jit
def kernel(x_prompt, x_sample, mem_prompt, cache_sb_k, cache_sb_v, cache_swa_k, cache_swa_v,
           cache_mem_k, cache_mem_v, pre_norm, post_norm, mem_norm, w_in_a, w_in_b, w_mem_kv,
           w_out, rel_bias, sinks):
    batch, seq, d = x_prompt.shape
    dec_batch, t_new, _ = x_sample.shape
    depth = pre_norm.shape[0]
    past = cache_sb_k.shape[2]
    n_p, n_s = batch * seq, dec_batch * t_new
    d_inner = w_out.shape[1]

    xp = x_prompt.reshape(n_p, d)
    xs = x_sample.reshape(n_s, d)
    proj_dtypes = (BF16, F32, F32, BF16, F32)
    widths_a = (SB_W, SB_W, SB_W, MEM_W, d_inner)
    widths_b = (SWA_W, SWA_KV_W, SWA_KV_W, MEM_W, d_inner)

    mem_k, mem_v = _mem_proj(mem_prompt.reshape(batch * MEM_LEN, d), mem_norm,
                             w_mem_kv.astype(BF16), tm=512)
    bias = _band_bias(rel_bias)
    zero_front = jnp.zeros((batch * WINDOW, SWA_KV_W), F32)

    sb_kp, sb_vp, sb_ks, sb_vs = [], [], [], []
    swa_kp, swa_vp, swa_ks, swa_vs = [], [], [], []
    for i in range(depth):
        j = i // 2
        if i % 2 == 0:
            weights = _split_cols(w_in_a[j], widths_a)
            q_p, k_p, v_p, qm_p, z_p = _norm_proj(xp, pre_norm[i], weights, proj_dtypes, tm=512)
            q_s, k_s, v_s, qm_s, z_s = _norm_proj(xs, pre_norm[i], weights, proj_dtypes, tm=256)
            o_p = _sb_prompt(q_p, k_p, v_p, batch, seq)
            o_s = _sb_sample(q_s, k_s, v_s, cache_sb_k[j].reshape(dec_batch * past, SB_W),
                             cache_sb_v[j].reshape(dec_batch * past, SB_W), dec_batch, t_new, past)
            sb_kp.append(k_p)
            sb_vp.append(v_p)
            sb_ks.append(k_s)
            sb_vs.append(v_s)
        else:
            weights = _split_cols(w_in_b[j], widths_b)
            q_p, k_p, v_p, qm_p, z_p = _norm_proj(xp, pre_norm[i], weights, proj_dtypes, tm=512)
            q_s, k_s, v_s, qm_s, z_s = _norm_proj(xs, pre_norm[i], weights, proj_dtypes, tm=256)
            sink = _sink_rows(sinks[j])
            o_p = _swa(q_p, k_p, v_p, zero_front, zero_front, bias, sink, batch, seq, front_valid=False)
            kc = cache_swa_k[j].reshape(dec_batch * WINDOW, SWA_KV_W)
            vc = cache_swa_v[j].reshape(dec_batch * WINDOW, SWA_KV_W)
            o_s = _swa(q_s, k_s, v_s, kc, vc, bias, sink, dec_batch, t_new, front_valid=True)
            swa_kp.append(k_p.reshape(batch, seq, SWA_KV_W)[:, seq - WINDOW:])
            swa_vp.append(v_p.reshape(batch, seq, SWA_KV_W)[:, seq - WINDOW:])
            k_all = jnp.concatenate([kc.reshape(dec_batch, WINDOW, SWA_KV_W),
                                     k_s.reshape(dec_batch, t_new, SWA_KV_W)], axis=1)
            v_all = jnp.concatenate([vc.reshape(dec_batch, WINDOW, SWA_KV_W),
                                     v_s.reshape(dec_batch, t_new, SWA_KV_W)], axis=1)
            swa_ks.append(k_all[:, t_new:])
            swa_vs.append(v_all[:, t_new:])
        om_p = _mem_attn(qm_p, mem_k[i], mem_v[i], batch, seq, tq=512)
        om_s = _mem_attn(qm_s, cache_mem_k[i].reshape(dec_batch * MEM_LEN, MEM_W),
                         cache_mem_v[i].reshape(dec_batch * MEM_LEN, MEM_W), dec_batch, t_new, tq=t_new)
        w_o = w_out[i].astype(BF16)
        xp = _out_proj(o_p, om_p, z_p, xp, w_o, post_norm[i], tm=512)
        xs = _out_proj(o_s, om_s, z_s, xs, w_o, post_norm[i], tm=256)

    n_a, n_b = len(sb_kp), len(swa_kp)
    return (xp.reshape(batch, seq, d), xs.reshape(dec_batch, t_new, d),
            jnp.stack(sb_kp).reshape(n_a, batch, seq, SB_HEADS, SB_HD),
            jnp.stack(sb_vp).reshape(n_a, batch, seq, SB_HEADS, SB_HD),
            jnp.stack(sb_ks).reshape(n_a, dec_batch, t_new, SB_HEADS, SB_HD),
            jnp.stack(sb_vs).reshape(n_a, dec_batch, t_new, SB_HEADS, SB_HD),
            jnp.stack(swa_kp).reshape(n_b, batch, WINDOW, SWA_KV_HEADS, SWA_HD),
            jnp.stack(swa_vp).reshape(n_b, batch, WINDOW, SWA_KV_HEADS, SWA_HD),
            jnp.stack(swa_ks).reshape(n_b, dec_batch, WINDOW, SWA_KV_HEADS, SWA_HD),
            jnp.stack(swa_vs).reshape(n_b, dec_batch, WINDOW, SWA_KV_HEADS, SWA_HD),
            mem_k.reshape(depth, batch, MEM_LEN, MEM_HEADS, MEM_HD),
            mem_v.reshape(depth, batch, MEM_LEN, MEM_HEADS, MEM_HD))
```

```python
import functools
import math

import numpy as np
import jax
import jax.numpy as jnp
from jax import lax
from jax.experimental import pallas as pl
from jax.experimental.pallas import tpu as pltpu

F32 = jnp.float32
BF16 = jnp.bfloat16

D_MODEL = 1024
CHUNK = 64
MEM_LEN = 256
MEM_HEADS = 4
MEM_HD = 128
MEM_W = MEM_HEADS * MEM_HD
SB_HEADS = 12
SB_HD = 128
SB_W = SB_HEADS * SB_HD
SWA_HEADS = 24
SWA_KV_HEADS = 3
SWA_GROUP = SWA_HEADS // SWA_KV_HEADS
SWA_HD = 64
SWA_W = SWA_HEADS * SWA_HD
SWA_KV_W = SWA_KV_HEADS * SWA_HD
WINDOW = 128
BAND = WINDOW + CHUNK
NUM_BUCKETS = 32
MAX_DISTANCE = 128
EPS = 1e-6
NEG_INF = -1e30

VMEM_LIMIT_BYTES = 56 * 1024 * 1024


def _params(semantics):
    return pltpu.CompilerParams(dimension_semantics=semantics,
                                vmem_limit_bytes=VMEM_LIMIT_BYTES)


def _resident(shape):
    nd = len(shape)
    return pl.BlockSpec(shape, lambda *_: (0,) * nd, pipeline_mode=pl.Buffered(1))


def _rmsnorm(x, g):
    return x * lax.rsqrt(jnp.mean(x * x, axis=-1, keepdims=True) + EPS) * g


def _dot(a, b):
    return jnp.dot(a, b, preferred_element_type=F32)


def _dot_nt(a, b):
    return lax.dot_general(a, b, (((1,), (1,)), ((), ())), preferred_element_type=F32)


PROJ_COL_CHUNK = 512


def _norm_proj_kernel(x_ref, g_ref, *refs, widths, state_idx, n_prev):
    n = len(widths)
    w_refs = refs[:n]
    o_refs = refs[n + n_prev:2 * n + n_prev]
    s_refs = refs[2 * n + n_prev:]
    h = _rmsnorm(x_ref[...], g_ref[...]).astype(BF16)
    for idx, (w_ref, o_ref, width) in enumerate(zip(w_refs, o_refs, widths)):
        s_ref = s_refs[state_idx.index(idx)] if idx in state_idx else None
        for c in range(0, width, PROJ_COL_CHUNK):
            cw = min(PROJ_COL_CHUNK, width - c)
            y = _dot(h, w_ref[:, c:c + cw])
            o_ref[:, c:c + cw] = y.astype(o_ref.dtype)
            if s_ref is not None:
                for hh in range(cw // SB_HD):
                    s_ref[0, :, c // SB_HD + hh, :] = y[:, hh * SB_HD:(hh + 1) * SB_HD]


def _norm_proj(x, gain, weights, dtypes, tm, state_idx=(), slab=0, n_slabs=1, prev_states=None):
    n_tok, d = x.shape
    widths = tuple(w.shape[1] for w in weights)
    prev = tuple(prev_states) if prev_states is not None else ()
    n_in = 2 + len(weights)
    state_shape = jax.ShapeDtypeStruct((n_slabs, n_tok, SB_HEADS, SB_HD), F32)
    outs = pl.pallas_call(
        functools.partial(_norm_proj_kernel, widths=widths, state_idx=tuple(state_idx), n_prev=len(prev)),
        grid=(n_tok // tm,),
        in_specs=[pl.BlockSpec((tm, d), lambda i: (i, 0)), _resident((1, d))]
        + [_resident(w.shape) for w in weights]
        + [pl.BlockSpec(memory_space=pl.ANY) for _ in prev],
        out_specs=[pl.BlockSpec((tm, w), lambda i: (i, 0)) for w in widths]
        + [pl.BlockSpec((1, tm, SB_HEADS, SB_HD), lambda i: (slab, i, 0, 0)) for _ in state_idx],
        out_shape=[jax.ShapeDtypeStruct((n_tok, w), dt) for w, dt in zip(widths, dtypes)]
        + [state_shape for _ in state_idx],
        input_output_aliases={n_in + s: len(widths) + s for s in range(len(prev))},
        compiler_params=_params(("parallel",)),
        name="norm_proj",
    )(x, gain.reshape(1, d), *weights, *prev)
    return outs[:len(widths)], outs[len(widths):]


def _mem_proj_kernel(x_ref, g_ref, w_ref, k_ref, v_ref):
    h = _rmsnorm(x_ref[...], g_ref[0]).astype(BF16)
    k_ref[0] = _dot(h, w_ref[0, :, :MEM_W])
    v_ref[0] = _dot(h, w_ref[0, :, MEM_W:])


def _mem_proj(mem, gains, w, tm):
    n_tok, d = mem.shape
    depth = w.shape[0]
    out = jax.ShapeDtypeStruct((depth, n_tok, MEM_W), F32)
    return pl.pallas_call(
        _mem_proj_kernel,
        grid=(depth, n_tok // tm),
        in_specs=[pl.BlockSpec((tm, d), lambda l, i: (i, 0)),
                  pl.BlockSpec((1, 1, d), lambda l, i: (l, 0, 0)),
                  pl.BlockSpec((1, d, 2 * MEM_W), lambda l, i: (l, 0, 0))],
        out_specs=[pl.BlockSpec((1, tm, MEM_W), lambda l, i: (l, i, 0))] * 2,
        out_shape=[out, out],
        compiler_params=_params(("parallel", "parallel")),
        name="mem_proj",
    )(mem, gains.reshape(depth, 1, d), w)


LOG2E = math.log2(math.e)
SB_LOG2_SCALE = SB_HD ** -0.5 * LOG2E
SB_TQ = 256
SB_TK = 256
SB_HEADS_PER_STEP = 6
SB_SAMPLE_HEADS_PER_STEP = 3
SB_DEAD_LOG2 = 151.0


def _later_key_matrix(n):
    later = lax.broadcasted_iota(jnp.int32, (n, n), 0) > lax.broadcasted_iota(jnp.int32, (n, n), 1)
    return jnp.where(later, 1.0, 0.0).astype(BF16)


def _strict_causal(n):
    return lax.broadcasted_iota(jnp.int32, (n, n), 1) < lax.broadcasted_iota(jnp.int32, (n, n), 0)


def _sb_visit(heads, q_ref, load_k, load_v, later, state, causal):
    cols = [slice(c * SB_HD, (c + 1) * SB_HD) for c in range(heads)]
    z2 = [_dot_nt(q_ref[:, cs], load_k(cs)) * SB_LOG2_SCALE for cs in cols]
    nl = [jnp.maximum(z, 0.0) + jnp.log2(1.0 + jnp.exp2(-jnp.abs(z))) for z in z2]
    if causal is not None:
        nl = [jnp.where(causal, x, 0.0) for x in nl]
    hi = [x.astype(BF16) for x in nl]
    lo = [(x - h.astype(F32)).astype(BF16) for x, h in zip(nl, hi)]
    nsuffix = [_dot(h, later) + _dot(l, later) + st[0] for h, l, st in zip(hi, lo, state)]
    a = [jnp.exp2(z - x - s) for z, x, s in zip(z2, nl, nsuffix)]
    if causal is not None:
        a = [jnp.where(causal, x, 0.0) for x in a]
    acc = [st[1] + _dot(x.astype(BF16), load_v(cs)) for x, cs, st in zip(a, cols, state)]
    ncarry = [st[0] + jnp.sum(x, axis=-1, keepdims=True) for x, st in zip(nl, state)]
    return tuple(zip(ncarry, acc))


def _sb_older_blocks(n_blocks, visit, state):
    def cond(st):
        return jnp.logical_and(st[0] < n_blocks, st[1] > 0)

    def body(st):
        new = visit(st[0], st[2])
        least = functools.reduce(jnp.minimum, [s[0] for s in new])
        alive = (jnp.min(least) < SB_DEAD_LOG2).astype(jnp.int32)
        return st[0] + 1, alive, new

    return lax.while_loop(cond, body, (jnp.int32(0), jnp.int32(1), state))[2]


def _sb_init(heads, tq):
    return tuple((jnp.zeros((tq, 1), F32), jnp.zeros((tq, SB_HD), F32)) for _ in range(heads))


def _sb_prompt_kernel(q_ref, k_ref, v_ref, o_ref, *, heads):
    i = pl.program_id(2)
    later = _later_key_matrix(SB_TK)

    def visit_at(block, state, causal):
        rows = pl.ds(pl.multiple_of(block * SB_TK, SB_TK), SB_TK)
        return _sb_visit(heads, q_ref, lambda cols: k_ref[rows, cols], lambda cols: v_ref[rows, cols],
                         later, state, causal)

    state = visit_at(i, _sb_init(heads, SB_TQ), _strict_causal(SB_TQ))
    state = _sb_older_blocks(i, lambda jj, st: visit_at(i - 1 - jj, st, None), state)
    for c in range(heads):
        o_ref[:, c * SB_HD:(c + 1) * SB_HD] = state[c][1]


def _sb_prompt(q, k, v, batch, seq):
    n_tok = batch * seq
    nq = seq // SB_TQ
    heads = SB_HEADS_PER_STEP
    width = heads * SB_HD
    return pl.pallas_call(
        functools.partial(_sb_prompt_kernel, heads=heads),
        grid=(batch, SB_HEADS // heads, nq),
        in_specs=[pl.BlockSpec((SB_TQ, width), lambda b, h, i: (b * nq + i, h)),
                  pl.BlockSpec((seq, width), lambda b, h, i: (b, h)),
                  pl.BlockSpec((seq, width), lambda b, h, i: (b, h))],
        out_specs=pl.BlockSpec((SB_TQ, width), lambda b, h, i: (b * nq + i, h)),
        out_shape=jax.ShapeDtypeStruct((n_tok, SB_W), F32),
        compiler_params=_params(("parallel", "parallel", "parallel")),
        name="sb_prompt",
    )(q, k, v)


def _sb_sample_kernel(q_ref, kn_ref, vn_ref, kc_ref, vc_ref, o_ref, *, heads, t_new, past):
    state = _sb_visit(heads, q_ref, lambda cols: kn_ref[:, cols].astype(BF16),
                      lambda cols: vn_ref[:, cols].astype(BF16), _later_key_matrix(t_new),
                      _sb_init(heads, t_new), _strict_causal(t_new))
    later = _later_key_matrix(SB_TK)
    n_blocks = past // SB_TK

    def visit(jj, st):
        rows = pl.ds(pl.multiple_of((n_blocks - 1 - jj) * SB_TK, SB_TK), SB_TK)
        return _sb_visit(heads, q_ref, lambda cols: kc_ref[rows, cols].astype(BF16),
                         lambda cols: vc_ref[rows, cols].astype(BF16), later, st, None)

    state = _sb_older_blocks(n_blocks, visit, state)
    for c in range(heads):
        o_ref[:, c * SB_HD:(c + 1) * SB_HD] = state[c][1]


def _sb_sample(q, k_new, v_new, k_cache, v_cache, batch, t_new, past):
    heads = SB_SAMPLE_HEADS_PER_STEP
    width = heads * SB_HD
    new_spec = pl.BlockSpec((t_new, width), lambda b, h: (b, h))
    cache_spec = pl.BlockSpec((past, width), lambda b, h: (b, h))
    return pl.pallas_call(
        functools.partial(_sb_sample_kernel, heads=heads, t_new=t_new, past=past),
        grid=(batch, SB_HEADS // heads),
        in_specs=[new_spec, new_spec, new_spec, cache_spec, cache_spec],
        out_specs=new_spec,
        out_shape=jax.ShapeDtypeStruct((batch * t_new, SB_W), F32),
        compiler_params=_params(("parallel", "parallel")),
        name="sb_sample",
    )(q, k_new, v_new, k_cache, v_cache)


SWA_SCALE = SWA_HD ** -0.5
SWA_PAIRS = SWA_GROUP // 2
SWA_PAIR_W = 2 * SWA_HD
SWA_COLS = SWA_PAIRS * CHUNK
SWA_BIAS_VARIANTS = 3


def _swa_fill(dst_ref, front_ref, body_ref, seq):
    for g in range(SWA_KV_HEADS):
        for (lo, hi), src_ref in (((0, WINDOW), front_ref), ((WINDOW, WINDOW + seq), body_ref)):
            x = src_ref[:, g * SWA_HD:(g + 1) * SWA_HD]
            zero = jnp.zeros_like(x)
            dst_ref[g, 0, lo:hi, :] = jnp.concatenate([x, zero], axis=1).astype(BF16)
            dst_ref[g, 1, lo:hi, :] = jnp.concatenate([zero, x], axis=1).astype(BF16)


def _swa_softmax(s, sk):
    m = jnp.maximum(jnp.max(s, axis=0, keepdims=True), sk)
    p = jnp.exp(s - m)
    den = jnp.sum(p, axis=0, keepdims=True) + jnp.exp(sk - m)
    return p * (1.0 / den)


def _swa_kernel(q_ref, k_ref, v_ref, kf_ref, vf_ref, bias_ref, sink_ref, o_ref, kab_ref, vab_ref,
                *, chunks, seq, front_valid):
    t = pl.program_id(1)

    @pl.when(t == 0)
    def _():
        _swa_fill(kab_ref, kf_ref, k_ref, seq)
        _swa_fill(vab_ref, vf_ref, v_ref, seq)

    blocks = [(c, g) for c in range(chunks) for g in range(SWA_KV_HEADS)]

    def band(ref, c, g):
        rows = pl.ds(pl.multiple_of((t * chunks + c) * CHUNK, CHUNK), BAND)
        return jnp.concatenate([ref[g, 0, rows, :], ref[g, 1, rows, :]], axis=0)

    def q_pairs(c, g):
        return jnp.concatenate(
            [q_ref[c * CHUNK:(c + 1) * CHUNK, (g * SWA_PAIRS + r) * SWA_PAIR_W:(g * SWA_PAIRS + r + 1) * SWA_PAIR_W]
             for r in range(SWA_PAIRS)], axis=0)

    def variant(c):
        return SWA_BIAS_VARIANTS - 1 if front_valid else jnp.minimum(t * chunks + c, SWA_BIAS_VARIANTS - 1)

    s = [_dot_nt(band(kab_ref, c, g), q_pairs(c, g)) * SWA_SCALE + bias_ref[variant(c), g] for c, g in blocks]
    p = [jnp.concatenate([_swa_softmax(x[:BAND], sink_ref[g, 0]), _swa_softmax(x[BAND:], sink_ref[g, 1])],
                         axis=0).astype(BF16) for x, (c, g) in zip(s, blocks)]
    o = [lax.dot_general(band(vab_ref, c, g), x, (((0,), (0,)), ((), ())), preferred_element_type=F32).T
         for x, (c, g) in zip(p, blocks)]
    for x, (c, g) in zip(o, blocks):
        for r in range(SWA_PAIRS):
            lanes = slice((g * SWA_PAIRS + r) * SWA_PAIR_W, (g * SWA_PAIRS + r + 1) * SWA_PAIR_W)
            o_ref[c * CHUNK:(c + 1) * CHUNK, lanes] = x[r * CHUNK:(r + 1) * CHUNK]


def _swa(q, k, v, k_front, v_front, bias, sink, batch, seq, front_valid):
    chunks = min(4, seq // CHUNK)
    tile = chunks * CHUNK
    nt = seq // tile
    kv_spec = pl.BlockSpec((seq, SWA_KV_W), lambda b, t: (b, 0))
    front_spec = pl.BlockSpec((WINDOW, SWA_KV_W), lambda b, t: (b, 0))
    return pl.pallas_call(
        functools.partial(_swa_kernel, chunks=chunks, seq=seq, front_valid=front_valid),
        grid=(batch, nt),
        in_specs=[pl.BlockSpec((tile, SWA_W), lambda b, t: (b * nt + t, 0)),
                  kv_spec, kv_spec, front_spec, front_spec,
                  _resident(bias.shape), _resident(sink.shape)],
        out_specs=pl.BlockSpec((tile, SWA_W), lambda b, t: (b * nt + t, 0)),
        out_shape=jax.ShapeDtypeStruct((batch * seq, SWA_W), F32),
        scratch_shapes=[pltpu.VMEM((SWA_KV_HEADS, 2, seq + WINDOW, SWA_PAIR_W), BF16)] * 2,
        compiler_params=_params(("parallel", "arbitrary")),
        name="swa",
    )(q, k, v, k_front, v_front, bias, sink)


def _t5_bucket(rel):
    nb = NUM_BUCKETS // 2
    max_exact = nb // 2
    ret = np.where(rel > 0, nb, 0)
    n = np.abs(rel)
    nf = np.maximum(n, 1).astype(np.float32)
    large = max_exact + (np.log(nf / np.float32(max_exact)) / np.float32(math.log(MAX_DISTANCE / max_exact))
                         * np.float32(nb - max_exact)).astype(np.int32)
    large = np.minimum(large, nb - 1)
    return ret + np.where(n < max_exact, n, large)


def _band_bias(rel_bias):
    rel = np.arange(BAND)[None, :] - WINDOW - np.arange(CHUNK)[:, None]
    b = jnp.take(rel_bias.astype(F32), jnp.asarray(_t5_bucket(rel), jnp.int32), axis=0)
    b = b.reshape(CHUNK, BAND, SWA_KV_HEADS, SWA_PAIRS, 2)
    b = jnp.transpose(b, (2, 4, 1, 3, 0)).reshape(SWA_KV_HEADS, 2 * BAND, SWA_COLS)
    first_valid = np.array([WINDOW - v * CHUNK for v in range(SWA_BIAS_VARIANTS)])
    key = np.tile(np.arange(BAND), 2)
    mask = np.where(key[None, :] < first_valid[:, None], NEG_INF, 0.0).astype(np.float32)
    return b[None] + jnp.asarray(mask)[:, None, :, None]


def _sink_cols(sink):
    s = jnp.transpose(sink.astype(F32).reshape(SWA_KV_HEADS, SWA_PAIRS, 2), (0, 2, 1))
    return jnp.repeat(s, CHUNK, axis=-1).reshape(SWA_KV_HEADS, 2, 1, SWA_COLS)


MEM_SCALE = MEM_HD ** -0.5


def _mem_attn_kernel(q_ref, k_ref, v_ref, o_ref):
    for h in range(MEM_HEADS):
        cols = slice(h * MEM_HD, (h + 1) * MEM_HD)
        s = _dot_nt(q_ref[:, cols], k_ref[:, cols].astype(BF16)) * MEM_SCALE
        p = jnp.exp(s - jnp.max(s, axis=-1, keepdims=True))
        p = p / jnp.sum(p, axis=-1, keepdims=True)
        o_ref[:, cols] = _dot(p.astype(BF16), v_ref[:, cols].astype(BF16))


def _mem_attn(qm, mk, mv, batch, seq, tq):
    nt = seq // tq
    kv_spec = pl.BlockSpec((MEM_LEN, MEM_W), lambda b, t: (b, 0))
    return pl.pallas_call(
        _mem_attn_kernel,
        grid=(batch, nt),
        in_specs=[pl.BlockSpec((tq, MEM_W), lambda b, t: (b * nt + t, 0)), kv_spec, kv_spec],
        out_specs=pl.BlockSpec((tq, MEM_W), lambda b, t: (b * nt + t, 0)),
        out_shape=jax.ShapeDtypeStruct((batch * seq, MEM_W), F32),
        compiler_params=_params(("parallel", "parallel")),
        name="mem_attn",
    )(qm, mk, mv)


def _silu(z):
    return z * (1.0 / (1.0 + jnp.exp(-z)))


def _out_proj_kernel(o_ref, om_ref, z_ref, x_ref, w_ref, g_ref, y_ref, *, mix_w):
    u1 = (o_ref[...] * _silu(z_ref[:, :mix_w])).astype(BF16)
    u2 = (om_ref[...] * _silu(z_ref[:, mix_w:])).astype(BF16)
    y = _dot(u1, w_ref[:mix_w, :]) + _dot(u2, w_ref[mix_w:, :])
    y_ref[...] = x_ref[...] + _rmsnorm(y, g_ref[...])


def _out_proj(o, om, z, x, w, gain, tm):
    n_tok, d = x.shape
    mix_w = o.shape[1]
    row = lambda width: pl.BlockSpec((tm, width), lambda i: (i, 0))
    return pl.pallas_call(
        functools.partial(_out_proj_kernel, mix_w=mix_w),
        grid=(n_tok // tm,),
        in_specs=[row(mix_w), row(MEM_W), row(mix_w + MEM_W), row(d),
                  _resident(w.shape), _resident((1, d))],
        out_specs=row(d),
        out_shape=jax.ShapeDtypeStruct((n_tok, d), F32),
        compiler_params=_params(("parallel",)),
        name="out_proj",
    )(o, om, z, x, w, gain.reshape(1, d))


def _split_cols(w, widths):
    out, off = [], 0
    for width in widths:
        out.append(w[:, off:off + width].astype(BF16))
        off += width
    return out


@jax.jit
def kernel(x_prompt, x_sample, mem_prompt, cache_sb_k, cache_sb_v, cache_swa_k, cache_swa_v,
           cache_mem_k, cache_mem_v, pre_norm, post_norm, mem_norm, w_in_a, w_in_b, w_mem_kv,
           w_out, rel_bias, sinks):
    batch, seq, d = x_prompt.shape
    dec_batch, t_new, _ = x_sample.shape
    depth = pre_norm.shape[0]
    past = cache_sb_k.shape[2]
    n_p, n_s = batch * seq, dec_batch * t_new
    d_inner = w_out.shape[1]

    xp = x_prompt.reshape(n_p, d)
    xs = x_sample.reshape(n_s, d)
    proj_dtypes = (BF16, F32, F32, BF16, F32)
    widths_a = (SB_W, SB_W, SB_W, MEM_W, d_inner)
    widths_b = (SWA_W, SWA_KV_W, SWA_KV_W, MEM_W, d_inner)

    mem_k, mem_v = _mem_proj(mem_prompt.reshape(batch * MEM_LEN, d), mem_norm,
                             w_mem_kv.astype(BF16), tm=512)
    bias = _band_bias(rel_bias)
    zero_front = jnp.zeros((batch * WINDOW, SWA_KV_W), F32)

    n_a, n_b = (depth + 1) // 2, depth // 2
    sb_state_p = None
    sb_ks, sb_vs = [], []
    swa_kp, swa_vp, swa_ks, swa_vs = [], [], [], []
    for i in range(depth):
        j = i // 2
        if i % 2 == 0:
            weights = _split_cols(w_in_a[j], widths_a)
            (q_p, k_p, v_p, qm_p, z_p), sb_state_p = _norm_proj(
                xp, pre_norm[i], weights, (BF16, BF16, BF16, BF16, F32), tm=512,
                state_idx=(1, 2), slab=j, n_slabs=n_a, prev_states=sb_state_p)
            (q_s, k_s, v_s, qm_s, z_s), _ = _norm_proj(xs, pre_norm[i], weights, proj_dtypes, tm=256)
            o_p = _sb_prompt(q_p, k_p, v_p, batch, seq)
            o_s = _sb_sample(q_s, k_s, v_s, cache_sb_k[j].reshape(dec_batch * past, SB_W),
                             cache_sb_v[j].reshape(dec_batch * past, SB_W), dec_batch, t_new, past)
            sb_ks.append(k_s)
            sb_vs.append(v_s)
        else:
            weights = _split_cols(w_in_b[j], widths_b)
            (q_p, k_p, v_p, qm_p, z_p), _ = _norm_proj(xp, pre_norm[i], weights, proj_dtypes, tm=512)
            (q_s, k_s, v_s, qm_s, z_s), _ = _norm_proj(xs, pre_norm[i], weights, proj_dtypes, tm=256)
            sink = _sink_cols(sinks[j])
            o_p = _swa(q_p, k_p, v_p, zero_front, zero_front, bias, sink, batch, seq, front_valid=False)
            kc = cache_swa_k[j].reshape(dec_batch * WINDOW, SWA_KV_W)
            vc = cache_swa_v[j].reshape(dec_batch * WINDOW, SWA_KV_W)
            o_s = _swa(q_s, k_s, v_s, kc, vc, bias, sink, dec_batch, t_new, front_valid=True)
            swa_kp.append(k_p.reshape(batch, seq, SWA_KV_W)[:, seq - WINDOW:])
            swa_vp.append(v_p.reshape(batch, seq, SWA_KV_W)[:, seq - WINDOW:])
            k_all = jnp.concatenate([kc.reshape(dec_batch, WINDOW, SWA_KV_W),
                                     k_s.reshape(dec_batch, t_new, SWA_KV_W)], axis=1)
            v_all = jnp.concatenate([vc.reshape(dec_batch, WINDOW, SWA_KV_W),
                                     v_s.reshape(dec_batch, t_new, SWA_KV_W)], axis=1)
            swa_ks.append(k_all[:, t_new:])
            swa_vs.append(v_all[:, t_new:])
        om_p = _mem_attn(qm_p, mem_k[i], mem_v[i], batch, seq, tq=512)
        om_s = _mem_attn(qm_s, cache_mem_k[i].reshape(dec_batch * MEM_LEN, MEM_W),
                         cache_mem_v[i].reshape(dec_batch * MEM_LEN, MEM_W), dec_batch, t_new, tq=t_new)
        w_o = w_out[i].astype(BF16)
        xp = _out_proj(o_p, om_p, z_p, xp, w_o, post_norm[i], tm=512)
        xs = _out_proj(o_s, om_s, z_s, xs, w_o, post_norm[i], tm=256)

    return (xp.reshape(batch, seq, d), xs.reshape(dec_batch, t_new, d),
            sb_state_p[0].reshape(n_a, batch, seq, SB_HEADS, SB_HD),
            sb_state_p[1].reshape(n_a, batch, seq, SB_HEADS, SB_HD),
            jnp.stack(sb_ks).reshape(n_a, dec_batch, t_new, SB_HEADS, SB_HD),
            jnp.stack(sb_vs).reshape(n_a, dec_batch, t_new, SB_HEADS, SB_HD),
            jnp.stack(swa_kp).reshape(n_b, batch, WINDOW, SWA_KV_HEADS, SWA_HD),
            jnp.stack(swa_vp).reshape(n_b, batch, WINDOW, SWA_KV_HEADS, SWA_HD),
            jnp.stack(swa_ks).reshape(n_b, dec_batch, WINDOW, SWA_KV_HEADS, SWA_HD),
            jnp.stack(swa_vs).reshape(n_b, dec_batch, WINDOW, SWA_KV_HEADS, SWA_HD),
            mem_k.reshape(depth, batch, MEM_LEN, MEM_HEADS, MEM_HD),
            mem_v.reshape(depth, batch, MEM_LEN, MEM_HEADS, MEM_HD))
```

```python
import functools
import math

import numpy as np
import jax
import jax.numpy as jnp
from jax import lax
from jax.experimental import pallas as pl
from jax.experimental.pallas import tpu as pltpu

F32 = jnp.float32
BF16 = jnp.bfloat16

D_MODEL = 1024
CHUNK = 64
MEM_LEN = 256
MEM_HEADS = 4
MEM_HD = 128
MEM_W = MEM_HEADS * MEM_HD
SB_HEADS = 12
SB_HD = 128
SB_W = SB_HEADS * SB_HD
SWA_HEADS = 24
SWA_KV_HEADS = 3
SWA_GROUP = SWA_HEADS // SWA_KV_HEADS
SWA_HD = 64
SWA_W = SWA_HEADS * SWA_HD
SWA_KV_W = SWA_KV_HEADS * SWA_HD
WINDOW = 128
BAND = WINDOW + CHUNK
NUM_BUCKETS = 32
MAX_DISTANCE = 128
EPS = 1e-6
NEG_INF = -1e30

VMEM_LIMIT_BYTES = 56 * 1024 * 1024


def _params(semantics):
    return pltpu.CompilerParams(dimension_semantics=semantics,
                                vmem_limit_bytes=VMEM_LIMIT_BYTES)


def _resident(shape):
    nd = len(shape)
    return pl.BlockSpec(shape, lambda *_: (0,) * nd, pipeline_mode=pl.Buffered(1))


def _rmsnorm(x, g):
    return x * lax.rsqrt(jnp.mean(x * x, axis=-1, keepdims=True) + EPS) * g


def _dot(a, b):
    return jnp.dot(a, b, preferred_element_type=F32)


def _dot_nt(a, b):
    return lax.dot_general(a, b, (((1,), (1,)), ((), ())), preferred_element_type=F32)


PROJ_COL_CHUNK = 512


def _silu(z):
    return z * (1.0 / (1.0 + jnp.exp(-z)))


def _norm_proj_kernel(x_ref, g_ref, *refs, widths, state_idx, gate_idx, n_prev):
    n = len(widths)
    w_refs = refs[:n]
    o_refs = refs[n + n_prev:2 * n + n_prev]
    s_refs = refs[2 * n + n_prev:]
    h = _rmsnorm(x_ref[...], g_ref[...]).astype(BF16)
    for idx, (w_ref, o_ref, width) in enumerate(zip(w_refs, o_refs, widths)):
        s_ref = s_refs[state_idx.index(idx)] if idx in state_idx else None
        for c in range(0, width, PROJ_COL_CHUNK):
            cw = min(PROJ_COL_CHUNK, width - c)
            y = _dot(h, w_ref[:, c:c + cw])
            o_ref[:, c:c + cw] = (_silu(y) if idx in gate_idx else y).astype(o_ref.dtype)
            if s_ref is not None:
                for hh in range(cw // SB_HD):
                    s_ref[0, 0, c // SB_HD + hh, :, :] = y[:, hh * SB_HD:(hh + 1) * SB_HD]


def _norm_proj(x, gain, weights, dtypes, tm, gate_idx=(), state_idx=(), slab=0, n_slabs=1, seq=None,
               prev_states=None):
    n_tok, d = x.shape
    widths = tuple(w.shape[1] for w in weights)
    prev = tuple(prev_states) if prev_states is not None else ()
    n_in = 2 + len(weights)
    state_specs, state_shapes = [], []
    if state_idx:
        nt = seq // tm
        state_specs = [pl.BlockSpec((1, 1, SB_HEADS, tm, SB_HD), lambda i: (slab, i // nt, 0, i % nt, 0))
                       for _ in state_idx]
        state_shapes = [jax.ShapeDtypeStruct((n_slabs, n_tok // seq, SB_HEADS, seq, SB_HD), F32)
                        for _ in state_idx]
    outs = pl.pallas_call(
        functools.partial(_norm_proj_kernel, widths=widths, state_idx=tuple(state_idx),
                          gate_idx=tuple(gate_idx), n_prev=len(prev)),
        grid=(n_tok // tm,),
        in_specs=[pl.BlockSpec((tm, d), lambda i: (i, 0)), _resident((1, d))]
        + [_resident(w.shape) for w in weights]
        + [pl.BlockSpec(memory_space=pl.ANY) for _ in prev],
        out_specs=[pl.BlockSpec((tm, w), lambda i: (i, 0)) for w in widths] + state_specs,
        out_shape=[jax.ShapeDtypeStruct((n_tok, w), dt) for w, dt in zip(widths, dtypes)] + state_shapes,
        input_output_aliases={n_in + s: len(widths) + s for s in range(len(prev))},
        compiler_params=_params(("parallel",)),
        name="norm_proj",
    )(x, gain.reshape(1, d), *weights, *prev)
    return outs[:len(widths)], outs[len(widths):]


def _mem_proj_kernel(x_ref, g_ref, w_ref, k_ref, v_ref):
    h = _rmsnorm(x_ref[...], g_ref[0]).astype(BF16)
    k_ref[0] = _dot(h, w_ref[0, :, :MEM_W])
    v_ref[0] = _dot(h, w_ref[0, :, MEM_W:])


def _mem_proj(mem, gains, w, tm):
    n_tok, d = mem.shape
    depth = w.shape[0]
    out = jax.ShapeDtypeStruct((depth, n_tok, MEM_W), F32)
    return pl.pallas_call(
        _mem_proj_kernel,
        grid=(depth, n_tok // tm),
        in_specs=[pl.BlockSpec((tm, d), lambda l, i: (i, 0)),
                  pl.BlockSpec((1, 1, d), lambda l, i: (l, 0, 0)),
                  pl.BlockSpec((1, d, 2 * MEM_W), lambda l, i: (l, 0, 0))],
        out_specs=[pl.BlockSpec((1, tm, MEM_W), lambda l, i: (l, i, 0))] * 2,
        out_shape=[out, out],
        compiler_params=_params(("parallel", "parallel")),
        name="mem_proj",
    )(mem, gains.reshape(depth, 1, d), w)


LOG2E = math.log2(math.e)
SB_LOG2_SCALE = SB_HD ** -0.5 * LOG2E
SB_TQ = 256
SB_TK = 256
SB_HEADS_PER_STEP = 6
SB_SAMPLE_HEADS_PER_STEP = 3
SB_DEAD_LOG2 = 151.0


def _later_key_matrix(n):
    later = lax.broadcasted_iota(jnp.int32, (n, n), 0) > lax.broadcasted_iota(jnp.int32, (n, n), 1)
    return jnp.where(later, 1.0, 0.0).astype(BF16)


def _strict_causal(n):
    return lax.broadcasted_iota(jnp.int32, (n, n), 1) < lax.broadcasted_iota(jnp.int32, (n, n), 0)


def _sb_visit(heads, q_ref, load_k, load_v, later, state, causal):
    cols = [slice(c * SB_HD, (c + 1) * SB_HD) for c in range(heads)]
    z2 = [_dot_nt(q_ref[:, cs], load_k(cs)) * SB_LOG2_SCALE for cs in cols]
    nl = [jnp.maximum(z, 0.0) + jnp.log2(1.0 + jnp.exp2(-jnp.abs(z))) for z in z2]
    if causal is not None:
        nl = [jnp.where(causal, x, 0.0) for x in nl]
    hi = [x.astype(BF16) for x in nl]
    lo = [(x - h.astype(F32)).astype(BF16) for x, h in zip(nl, hi)]
    nsuffix = [_dot(h, later) + _dot(l, later) + st[0] for h, l, st in zip(hi, lo, state)]
    a = [jnp.exp2(z - x - s) for z, x, s in zip(z2, nl, nsuffix)]
    if causal is not None:
        a = [jnp.where(causal, x, 0.0) for x in a]
    acc = [st[1] + _dot(x.astype(BF16), load_v(cs)) for x, cs, st in zip(a, cols, state)]
    ncarry = [st[0] + jnp.sum(x, axis=-1, keepdims=True) for x, st in zip(nl, state)]
    return tuple(zip(ncarry, acc))


def _sb_older_blocks(n_blocks, visit, state):
    def cond(st):
        return jnp.logical_and(st[0] < n_blocks, st[1] > 0)

    def body(st):
        new = visit(st[0], st[2])
        least = functools.reduce(jnp.minimum, [s[0] for s in new])
        alive = (jnp.min(least) < SB_DEAD_LOG2).astype(jnp.int32)
        return st[0] + 1, alive, new

    return lax.while_loop(cond, body, (jnp.int32(0), jnp.int32(1), state))[2]


def _sb_init(heads, tq):
    return tuple((jnp.zeros((tq, 1), F32), jnp.zeros((tq, SB_HD), F32)) for _ in range(heads))


def _sb_prompt_kernel(q_ref, k_ref, v_ref, o_ref, *, heads):
    i = pl.program_id(2)
    later = _later_key_matrix(SB_TK)

    def visit_at(block, state, causal):
        rows = pl.ds(pl.multiple_of(block * SB_TK, SB_TK), SB_TK)
        return _sb_visit(heads, q_ref, lambda cols: k_ref[rows, cols], lambda cols: v_ref[rows, cols],
                         later, state, causal)

    state = visit_at(i, _sb_init(heads, SB_TQ), _strict_causal(SB_TQ))
    state = _sb_older_blocks(i, lambda jj, st: visit_at(i - 1 - jj, st, None), state)
    for c in range(heads):
        o_ref[:, c * SB_HD:(c + 1) * SB_HD] = state[c][1].astype(o_ref.dtype)


def _sb_prompt(q, k, v, batch, seq):
    n_tok = batch * seq
    nq = seq // SB_TQ
    heads = SB_HEADS_PER_STEP
    width = heads * SB_HD
    return pl.pallas_call(
        functools.partial(_sb_prompt_kernel, heads=heads),
        grid=(batch, SB_HEADS // heads, nq),
        in_specs=[pl.BlockSpec((SB_TQ, width), lambda b, h, i: (b * nq + i, h)),
                  pl.BlockSpec((seq, width), lambda b, h, i: (b, h)),
                  pl.BlockSpec((seq, width), lambda b, h, i: (b, h))],
        out_specs=pl.BlockSpec((SB_TQ, width), lambda b, h, i: (b * nq + i, h)),
        out_shape=jax.ShapeDtypeStruct((n_tok, SB_W), BF16),
        compiler_params=_params(("parallel", "parallel", "parallel")),
        name="sb_prompt",
    )(q, k, v)


def _sb_sample_kernel(q_ref, kn_ref, vn_ref, kc_ref, vc_ref, o_ref, *, heads, t_new, past):
    state = _sb_visit(heads, q_ref, lambda cols: kn_ref[:, cols].astype(BF16),
                      lambda cols: vn_ref[:, cols].astype(BF16), _later_key_matrix(t_new),
                      _sb_init(heads, t_new), _strict_causal(t_new))
    later = _later_key_matrix(SB_TK)
    n_blocks = past // SB_TK

    def visit(jj, st):
        rows = pl.ds(pl.multiple_of((n_blocks - 1 - jj) * SB_TK, SB_TK), SB_TK)
        return _sb_visit(heads, q_ref, lambda cols: kc_ref[0, 0, cols.start // SB_HD, rows, :].astype(BF16),
                         lambda cols: vc_ref[0, 0, cols.start // SB_HD, rows, :].astype(BF16), later, st, None)

    state = _sb_older_blocks(n_blocks, visit, state)
    for c in range(heads):
        o_ref[:, c * SB_HD:(c + 1) * SB_HD] = state[c][1].astype(o_ref.dtype)


def _sb_sample(q, k_new, v_new, k_cache, v_cache, layer, t_new):
    _, batch, _, past, _ = k_cache.shape
    heads = SB_SAMPLE_HEADS_PER_STEP
    width = heads * SB_HD
    new_spec = pl.BlockSpec((t_new, width), lambda b, h: (b, h))
    cache_spec = pl.BlockSpec((1, 1, heads, past, SB_HD), lambda b, h: (layer, b, h, 0, 0))
    return pl.pallas_call(
        functools.partial(_sb_sample_kernel, heads=heads, t_new=t_new, past=past),
        grid=(batch, SB_HEADS // heads),
        in_specs=[new_spec, new_spec, new_spec, cache_spec, cache_spec],
        out_specs=new_spec,
        out_shape=jax.ShapeDtypeStruct((batch * t_new, SB_W), BF16),
        compiler_params=_params(("parallel", "parallel")),
        name="sb_sample",
    )(q, k_new, v_new, k_cache, v_cache)


SWA_SCALE = SWA_HD ** -0.5
SWA_PAIRS = SWA_GROUP // 2
SWA_PAIR_W = 2 * SWA_HD
SWA_COLS = SWA_PAIRS * CHUNK
SWA_BIAS_VARIANTS = 3


def _swa_fill(dst_ref, front_ref, body_ref, seq):
    for g in range(SWA_KV_HEADS):
        for (lo, hi), src_ref in (((0, WINDOW), front_ref), ((WINDOW, WINDOW + seq), body_ref)):
            x = src_ref[:, g * SWA_HD:(g + 1) * SWA_HD]
            zero = jnp.zeros_like(x)
            dst_ref[g, 0, lo:hi, :] = jnp.concatenate([x, zero], axis=1).astype(BF16)
            dst_ref[g, 1, lo:hi, :] = jnp.concatenate([zero, x], axis=1).astype(BF16)


def _swa_softmax(s, sk):
    m = jnp.maximum(jnp.max(s, axis=0, keepdims=True), sk)
    p = jnp.exp(s - m)
    den = jnp.sum(p, axis=0, keepdims=True) + jnp.exp(sk - m)
    return p * (1.0 / den)


def _swa_kernel(q_ref, k_ref, v_ref, kf_ref, vf_ref, bias_ref, sink_ref, o_ref, kab_ref, vab_ref,
                *, chunks, seq, front_valid):
    t = pl.program_id(1)

    @pl.when(t == 0)
    def _():
        _swa_fill(kab_ref, kf_ref, k_ref, seq)
        _swa_fill(vab_ref, vf_ref, v_ref, seq)

    blocks = [(c, g) for c in range(chunks) for g in range(SWA_KV_HEADS)]

    def band(ref, c, g):
        rows = pl.ds(pl.multiple_of((t * chunks + c) * CHUNK, CHUNK), BAND)
        return jnp.concatenate([ref[g, 0, rows, :], ref[g, 1, rows, :]], axis=0)

    def q_pairs(c, g):
        return jnp.concatenate(
            [q_ref[c * CHUNK:(c + 1) * CHUNK, (g * SWA_PAIRS + r) * SWA_PAIR_W:(g * SWA_PAIRS + r + 1) * SWA_PAIR_W]
             for r in range(SWA_PAIRS)], axis=0)

    def variant(c):
        return SWA_BIAS_VARIANTS - 1 if front_valid else jnp.minimum(t * chunks + c, SWA_BIAS_VARIANTS - 1)

    s = [_dot_nt(band(kab_ref, c, g), q_pairs(c, g)) * SWA_SCALE + bias_ref[variant(c), g] for c, g in blocks]
    p = [jnp.concatenate([_swa_softmax(x[:BAND], sink_ref[g, 0]), _swa_softmax(x[BAND:], sink_ref[g, 1])],
                         axis=0).astype(BF16) for x, (c, g) in zip(s, blocks)]
    o = [lax.dot_general(band(vab_ref, c, g), x, (((0,), (0,)), ((), ())), preferred_element_type=F32).T
         for x, (c, g) in zip(p, blocks)]
    for x, (c, g) in zip(o, blocks):
        for r in range(SWA_PAIRS):
            lanes = slice((g * SWA_PAIRS + r) * SWA_PAIR_W, (g * SWA_PAIRS + r + 1) * SWA_PAIR_W)
            o_ref[c * CHUNK:(c + 1) * CHUNK, lanes] = x[r * CHUNK:(r + 1) * CHUNK].astype(o_ref.dtype)


def _swa(q, k, v, k_front, v_front, bias, sink, batch, seq, front_valid):
    chunks = min(4, seq // CHUNK)
    tile = chunks * CHUNK
    nt = seq // tile
    kv_spec = pl.BlockSpec((seq, SWA_KV_W), lambda b, t: (b, 0))
    front_spec = pl.BlockSpec((WINDOW, SWA_KV_W), lambda b, t: (b, 0))
    return pl.pallas_call(
        functools.partial(_swa_kernel, chunks=chunks, seq=seq, front_valid=front_valid),
        grid=(batch, nt),
        in_specs=[pl.BlockSpec((tile, SWA_W), lambda b, t: (b * nt + t, 0)),
                  kv_spec, kv_spec, front_spec, front_spec,
                  _resident(bias.shape), _resident(sink.shape)],
        out_specs=pl.BlockSpec((tile, SWA_W), lambda b, t: (b * nt + t, 0)),
        out_shape=jax.ShapeDtypeStruct((batch * seq, SWA_W), BF16),
        scratch_shapes=[pltpu.VMEM((SWA_KV_HEADS, 2, seq + WINDOW, SWA_PAIR_W), BF16)] * 2,
        compiler_params=_params(("parallel", "arbitrary")),
        name="swa",
    )(q, k, v, k_front, v_front, bias, sink)


def _t5_bucket(rel):
    nb = NUM_BUCKETS // 2
    max_exact = nb // 2
    ret = np.where(rel > 0, nb, 0)
    n = np.abs(rel)
    nf = np.maximum(n, 1).astype(np.float32)
    large = max_exact + (np.log(nf / np.float32(max_exact)) / np.float32(math.log(MAX_DISTANCE / max_exact))
                         * np.float32(nb - max_exact)).astype(np.int32)
    large = np.minimum(large, nb - 1)
    return ret + np.where(n < max_exact, n, large)


def _band_bias(rel_bias):
    rel = np.arange(BAND)[None, :] - WINDOW - np.arange(CHUNK)[:, None]
    b = jnp.take(rel_bias.astype(F32), jnp.asarray(_t5_bucket(rel), jnp.int32), axis=0)
    b = b.reshape(CHUNK, BAND, SWA_KV_HEADS, SWA_PAIRS, 2)
    b = jnp.transpose(b, (2, 4, 1, 3, 0)).reshape(SWA_KV_HEADS, 2 * BAND, SWA_COLS)
    first_valid = np.array([WINDOW - v * CHUNK for v in range(SWA_BIAS_VARIANTS)])
    key = np.tile(np.arange(BAND), 2)
    mask = np.where(key[None, :] < first_valid[:, None], NEG_INF, 0.0).astype(np.float32)
    return b[None] + jnp.asarray(mask)[:, None, :, None]


def _sink_cols(sink):
    s = jnp.transpose(sink.astype(F32).reshape(SWA_KV_HEADS, SWA_PAIRS, 2), (0, 2, 1))
    return jnp.repeat(s, CHUNK, axis=-1).reshape(SWA_KV_HEADS, 2, 1, SWA_COLS)


MEM_SCALE = MEM_HD ** -0.5


def _mem_attn_kernel(q_ref, k_ref, v_ref, o_ref):
    for h in range(MEM_HEADS):
        cols = slice(h * MEM_HD, (h + 1) * MEM_HD)
        s = _dot_nt(q_ref[:, cols], k_ref[:, cols].astype(BF16)) * MEM_SCALE
        p = jnp.exp(s - jnp.max(s, axis=-1, keepdims=True))
        p = p / jnp.sum(p, axis=-1, keepdims=True)
        o_ref[:, cols] = _dot(p.astype(BF16), v_ref[:, cols].astype(BF16)).astype(o_ref.dtype)


def _mem_attn(qm, mk, mv, batch, seq, tq):
    nt = seq // tq
    kv_spec = pl.BlockSpec((MEM_LEN, MEM_W), lambda b, t: (b, 0))
    return pl.pallas_call(
        _mem_attn_kernel,
        grid=(batch, nt),
        in_specs=[pl.BlockSpec((tq, MEM_W), lambda b, t: (b * nt + t, 0)), kv_spec, kv_spec],
        out_specs=pl.BlockSpec((tq, MEM_W), lambda b, t: (b * nt + t, 0)),
        out_shape=jax.ShapeDtypeStruct((batch * seq, MEM_W), BF16),
        compiler_params=_params(("parallel", "parallel")),
        name="mem_attn",
    )(qm, mk, mv)


def _out_proj_kernel(o_ref, om_ref, gate_ref, x_ref, w_ref, g_ref, y_ref, *, mix_w):
    u1 = o_ref[...] * gate_ref[:, :mix_w]
    u2 = om_ref[...] * gate_ref[:, mix_w:]
    y = _dot(u1, w_ref[:mix_w, :]) + _dot(u2, w_ref[mix_w:, :])
    y_ref[...] = x_ref[...] + _rmsnorm(y, g_ref[...])


def _out_proj(o, om, gate, x, w, gain, tm):
    n_tok, d = x.shape
    mix_w = o.shape[1]
    row = lambda width: pl.BlockSpec((tm, width), lambda i: (i, 0))
    return pl.pallas_call(
        functools.partial(_out_proj_kernel, mix_w=mix_w),
        grid=(n_tok // tm,),
        in_specs=[row(mix_w), row(MEM_W), row(mix_w + MEM_W), row(d),
                  _resident(w.shape), _resident((1, d))],
        out_specs=row(d),
        out_shape=jax.ShapeDtypeStruct((n_tok, d), F32),
        compiler_params=_params(("parallel",)),
        name="out_proj",
    )(o, om, gate, x, w, gain.reshape(1, d))


def _split_cols(w, widths):
    out, off = [], 0
    for width in widths:
        out.append(w[:, off:off + width].astype(BF16))
        off += width
    return out


@jax.jit
def kernel(x_prompt, x_sample, mem_prompt, cache_sb_k, cache_sb_v, cache_swa_k, cache_swa_v,
           cache_mem_k, cache_mem_v, pre_norm, post_norm, mem_norm, w_in_a, w_in_b, w_mem_kv,
           w_out, rel_bias, sinks):
    batch, seq, d = x_prompt.shape
    dec_batch, t_new, _ = x_sample.shape
    depth = pre_norm.shape[0]
    past = cache_sb_k.shape[2]
    n_p, n_s = batch * seq, dec_batch * t_new
    d_inner = w_out.shape[1]

    xp = x_prompt.reshape(n_p, d)
    xs = x_sample.reshape(n_s, d)
    proj_dtypes = (BF16, F32, F32, BF16, BF16)
    gate = (4,)
    widths_a = (SB_W, SB_W, SB_W, MEM_W, d_inner)
    widths_b = (SWA_W, SWA_KV_W, SWA_KV_W, MEM_W, d_inner)

    mem_k, mem_v = _mem_proj(mem_prompt.reshape(batch * MEM_LEN, d), mem_norm,
                             w_mem_kv.astype(BF16), tm=512)
    bias = _band_bias(rel_bias)
    zero_front = jnp.zeros((batch * WINDOW, SWA_KV_W), F32)

    n_a, n_b = (depth + 1) // 2, depth // 2
    sb_state_p = None
    heads_major = (0, 1, 3, 2, 4)
    cache_k = jnp.transpose(cache_sb_k, heads_major)
    cache_v = jnp.transpose(cache_sb_v, heads_major)
    sb_ks, sb_vs = [], []
    swa_kp, swa_vp, swa_ks, swa_vs = [], [], [], []
    for i in range(depth):
        j = i // 2
        if i % 2 == 0:
            weights = _split_cols(w_in_a[j], widths_a)
            (q_p, k_p, v_p, qm_p, z_p), sb_state_p = _norm_proj(
                xp, pre_norm[i], weights, (BF16, BF16, BF16, BF16, BF16), tm=512, gate_idx=gate,
                state_idx=(1, 2), slab=j, n_slabs=n_a, seq=seq, prev_states=sb_state_p)
            (q_s, k_s, v_s, qm_s, z_s), _ = _norm_proj(xs, pre_norm[i], weights, proj_dtypes, tm=256,
                                                       gate_idx=gate)
            o_p = _sb_prompt(q_p, k_p, v_p, batch, seq)
            o_s = _sb_sample(q_s, k_s, v_s, cache_k, cache_v, j, t_new)
            sb_ks.append(k_s)
            sb_vs.append(v_s)
        else:
            weights = _split_cols(w_in_b[j], widths_b)
            (q_p, k_p, v_p, qm_p, z_p), _ = _norm_proj(xp, pre_norm[i], weights, proj_dtypes, tm=512,
                                                       gate_idx=gate)
            (q_s, k_s, v_s, qm_s, z_s), _ = _norm_proj(xs, pre_norm[i], weights, proj_dtypes, tm=256,
                                                       gate_idx=gate)
            sink = _sink_cols(sinks[j])
            o_p = _swa(q_p, k_p, v_p, zero_front, zero_front, bias, sink, batch, seq, front_valid=False)
            kc = cache_swa_k[j].reshape(dec_batch * WINDOW, SWA_KV_W)
            vc = cache_swa_v[j].reshape(dec_batch * WINDOW, SWA_KV_W)
            o_s = _swa(q_s, k_s, v_s, kc, vc, bias, sink, dec_batch, t_new, front_valid=True)
            swa_kp.append(k_p.reshape(batch, seq, SWA_KV_W)[:, seq - WINDOW:])
            swa_vp.append(v_p.reshape(batch, seq, SWA_KV_W)[:, seq - WINDOW:])
            k_all = jnp.concatenate([kc.reshape(dec_batch, WINDOW, SWA_KV_W),
                                     k_s.reshape(dec_batch, t_new, SWA_KV_W)], axis=1)
            v_all = jnp.concatenate([vc.reshape(dec_batch, WINDOW, SWA_KV_W),
                                     v_s.reshape(dec_batch, t_new, SWA_KV_W)], axis=1)
            swa_ks.append(k_all[:, t_new:])
            swa_vs.append(v_all[:, t_new:])
        om_p = _mem_attn(qm_p, mem_k[i], mem_v[i], batch, seq, tq=512)
        om_s = _mem_attn(qm_s, cache_mem_k[i].reshape(dec_batch * MEM_LEN, MEM_W),
                         cache_mem_v[i].reshape(dec_batch * MEM_LEN, MEM_W), dec_batch, t_new, tq=t_new)
        w_o = w_out[i].astype(BF16)
        xp = _out_proj(o_p, om_p, z_p, xp, w_o, post_norm[i], tm=512)
        xs = _out_proj(o_s, om_s, z_s, xs, w_o, post_norm[i], tm=256)

    return (xp.reshape(batch, seq, d), xs.reshape(dec_batch, t_new, d),
            jnp.transpose(sb_state_p[0], heads_major),
            jnp.transpose(sb_state_p[1], heads_major),
            jnp.stack(sb_ks).reshape(n_a, dec_batch, t_new, SB_HEADS, SB_HD),
            jnp.stack(sb_vs).reshape(n_a, dec_batch, t_new, SB_HEADS, SB_HD),
            jnp.stack(swa_kp).reshape(n_b, batch, WINDOW, SWA_KV_HEADS, SWA_HD),
            jnp.stack(swa_vp).reshape(n_b, batch, WINDOW, SWA_KV_HEADS, SWA_HD),
            jnp.stack(swa_ks).reshape(n_b, dec_batch, WINDOW, SWA_KV_HEADS, SWA_HD),
            jnp.stack(swa_vs).reshape(n_b, dec_batch, WINDOW, SWA_KV_HEADS, SWA_HD),
            mem_k.reshape(depth, batch, MEM_LEN, MEM_HEADS, MEM_HD),
            mem_v.reshape(depth, batch, MEM_LEN, MEM_HEADS, MEM_HD))
```

```python
import functools
import math

import numpy as np
import jax
import jax.numpy as jnp
from jax import lax
from jax.experimental import pallas as pl
from jax.experimental.pallas import tpu as pltpu

F32 = jnp.float32
BF16 = jnp.bfloat16

D_MODEL = 1024
CHUNK = 64
MEM_LEN = 256
MEM_HEADS = 4
MEM_HD = 128
MEM_W = MEM_HEADS * MEM_HD
SB_HEADS = 12
SB_HD = 128
SB_W = SB_HEADS * SB_HD
SWA_HEADS = 24
SWA_KV_HEADS = 3
SWA_GROUP = SWA_HEADS // SWA_KV_HEADS
SWA_HD = 64
SWA_W = SWA_HEADS * SWA_HD
SWA_KV_W = SWA_KV_HEADS * SWA_HD
WINDOW = 128
BAND = WINDOW + CHUNK
NUM_BUCKETS = 32
MAX_DISTANCE = 128
EPS = 1e-6
NEG_INF = -1e30

VMEM_LIMIT_BYTES = 56 * 1024 * 1024


def _params(semantics):
    return pltpu.CompilerParams(dimension_semantics=semantics,
                                vmem_limit_bytes=VMEM_LIMIT_BYTES)


def _resident(shape):
    nd = len(shape)
    return pl.BlockSpec(shape, lambda *_: (0,) * nd, pipeline_mode=pl.Buffered(1))


def _rmsnorm(x, g):
    return x * lax.rsqrt(jnp.mean(x * x, axis=-1, keepdims=True) + EPS) * g


def _dot(a, b):
    return jnp.dot(a, b, preferred_element_type=F32)


def _dot_nt(a, b):
    return lax.dot_general(a, b, (((1,), (1,)), ((), ())), preferred_element_type=F32)


PROJ_COL_CHUNK = 512


def _silu(z):
    return z * (1.0 / (1.0 + jnp.exp(-z)))


def _norm_proj_kernel(x_ref, g_ref, *refs, widths, state_idx, gate_idx, n_prev):
    n = len(widths)
    w_refs = refs[:n]
    o_refs = refs[n + n_prev:2 * n + n_prev]
    s_refs = refs[2 * n + n_prev:]
    h = _rmsnorm(x_ref[...], g_ref[...]).astype(BF16)
    for idx, (w_ref, o_ref, width) in enumerate(zip(w_refs, o_refs, widths)):
        s_ref = s_refs[state_idx.index(idx)] if idx in state_idx else None
        for c in range(0, width, PROJ_COL_CHUNK):
            cw = min(PROJ_COL_CHUNK, width - c)
            y = _dot(h, w_ref[:, c:c + cw])
            o_ref[:, c:c + cw] = (_silu(y) if idx in gate_idx else y).astype(o_ref.dtype)
            if s_ref is not None:
                for hh in range(cw // SB_HD):
                    s_ref[0, 0, c // SB_HD + hh, :, :] = y[:, hh * SB_HD:(hh + 1) * SB_HD]


def _norm_proj(x, gain, weights, dtypes, tm, gate_idx=(), state_idx=(), slab=0, n_slabs=1, seq=None,
               prev_states=None):
    n_tok, d = x.shape
    widths = tuple(w.shape[1] for w in weights)
    prev = tuple(prev_states) if prev_states is not None else ()
    n_in = 2 + len(weights)
    state_specs, state_shapes = [], []
    if state_idx:
        nt = seq // tm
        state_specs = [pl.BlockSpec((1, 1, SB_HEADS, tm, SB_HD), lambda i: (slab, i // nt, 0, i % nt, 0))
                       for _ in state_idx]
        state_shapes = [jax.ShapeDtypeStruct((n_slabs, n_tok // seq, SB_HEADS, seq, SB_HD), F32)
                        for _ in state_idx]
    outs = pl.pallas_call(
        functools.partial(_norm_proj_kernel, widths=widths, state_idx=tuple(state_idx),
                          gate_idx=tuple(gate_idx), n_prev=len(prev)),
        grid=(n_tok // tm,),
        in_specs=[pl.BlockSpec((tm, d), lambda i: (i, 0)), _resident((1, d))]
        + [_resident(w.shape) for w in weights]
        + [pl.BlockSpec(memory_space=pl.ANY) for _ in prev],
        out_specs=[pl.BlockSpec((tm, w), lambda i: (i, 0)) for w in widths] + state_specs,
        out_shape=[jax.ShapeDtypeStruct((n_tok, w), dt) for w, dt in zip(widths, dtypes)] + state_shapes,
        input_output_aliases={n_in + s: len(widths) + s for s in range(len(prev))},
        compiler_params=_params(("parallel",)),
        name="norm_proj",
    )(x, gain.reshape(1, d), *weights, *prev)
    return outs[:len(widths)], outs[len(widths):]


def _mem_proj_kernel(x_ref, g_ref, w_ref, k_ref, v_ref):
    h = _rmsnorm(x_ref[...], g_ref[0]).astype(BF16)
    k_ref[0] = _dot(h, w_ref[0, :, :MEM_W])
    v_ref[0] = _dot(h, w_ref[0, :, MEM_W:])


def _mem_proj(mem, gains, w, tm):
    n_tok, d = mem.shape
    depth = w.shape[0]
    out = jax.ShapeDtypeStruct((depth, n_tok, MEM_W), F32)
    return pl.pallas_call(
        _mem_proj_kernel,
        grid=(depth, n_tok // tm),
        in_specs=[pl.BlockSpec((tm, d), lambda l, i: (i, 0)),
                  pl.BlockSpec((1, 1, d), lambda l, i: (l, 0, 0)),
                  pl.BlockSpec((1, d, 2 * MEM_W), lambda l, i: (l, 0, 0))],
        out_specs=[pl.BlockSpec((1, tm, MEM_W), lambda l, i: (l, i, 0))] * 2,
        out_shape=[out, out],
        compiler_params=_params(("parallel", "parallel")),
        name="mem_proj",
    )(mem, gains.reshape(depth, 1, d), w)


LOG2E = math.log2(math.e)
SB_LOG2_SCALE = SB_HD ** -0.5 * LOG2E
SB_TQ = 256
SB_TK = 256
SB_HEADS_PER_STEP = 6
SB_SAMPLE_HEADS_PER_STEP = 3
SB_DEAD_LOG2 = 151.0


def _later_key_matrix(n):
    later = lax.broadcasted_iota(jnp.int32, (n, n), 0) > lax.broadcasted_iota(jnp.int32, (n, n), 1)
    return jnp.where(later, 1.0, 0.0).astype(BF16)


def _strict_causal(n):
    return lax.broadcasted_iota(jnp.int32, (n, n), 1) < lax.broadcasted_iota(jnp.int32, (n, n), 0)


def _sb_visit(heads, q_ref, load_k, load_v, later, state, causal):
    cols = [slice(c * SB_HD, (c + 1) * SB_HD) for c in range(heads)]
    z2 = [_dot_nt(q_ref[:, cs], load_k(cs)) * SB_LOG2_SCALE for cs in cols]
    nl = [jnp.maximum(z, 0.0) + jnp.log2(1.0 + jnp.exp2(-jnp.abs(z))) for z in z2]
    if causal is not None:
        nl = [jnp.where(causal, x, 0.0) for x in nl]
    ncarry = [st[0] + jnp.sum(x, axis=-1, keepdims=True) for x, st in zip(nl, state)]
    head = [z - x - st[0] for z, x, st in zip(z2, nl, state)]
    a = [jnp.exp2(t - _dot(x.astype(BF16), later)) for t, x in zip(head, nl)]
    if causal is not None:
        a = [jnp.where(causal, x, 0.0) for x in a]
    acc = [st[1] + _dot(x.astype(BF16), load_v(cs)) for x, cs, st in zip(a, cols, state)]
    return tuple(zip(ncarry, acc))


def _sb_older_blocks(n_blocks, visit, state):
    def cond(st):
        return jnp.logical_and(st[0] < n_blocks, st[1] > 0)

    def body(st):
        new = visit(st[0], st[2])
        least = functools.reduce(jnp.minimum, [s[0] for s in new])
        alive = (jnp.min(least) < SB_DEAD_LOG2).astype(jnp.int32)
        return st[0] + 1, alive, new

    return lax.while_loop(cond, body, (jnp.int32(0), jnp.int32(1), state))[2]


def _sb_init(heads, tq):
    return tuple((jnp.zeros((tq, 1), F32), jnp.zeros((tq, SB_HD), F32)) for _ in range(heads))


def _sb_prompt_kernel(q_ref, k_ref, v_ref, o_ref, *, heads):
    i = pl.program_id(2)
    later = _later_key_matrix(SB_TK)

    def visit_at(block, state, causal):
        rows = pl.ds(pl.multiple_of(block * SB_TK, SB_TK), SB_TK)
        return _sb_visit(heads, q_ref, lambda cols: k_ref[rows, cols], lambda cols: v_ref[rows, cols],
                         later, state, causal)

    state = visit_at(i, _sb_init(heads, SB_TQ), _strict_causal(SB_TQ))
    state = _sb_older_blocks(i, lambda jj, st: visit_at(i - 1 - jj, st, None), state)
    for c in range(heads):
        o_ref[:, c * SB_HD:(c + 1) * SB_HD] = state[c][1].astype(o_ref.dtype)


def _sb_prompt(q, k, v, batch, seq):
    n_tok = batch * seq
    nq = seq // SB_TQ
    heads = SB_HEADS_PER_STEP
    width = heads * SB_HD
    return pl.pallas_call(
        functools.partial(_sb_prompt_kernel, heads=heads),
        grid=(batch, SB_HEADS // heads, nq),
        in_specs=[pl.BlockSpec((SB_TQ, width), lambda b, h, i: (b * nq + i, h)),
                  pl.BlockSpec((seq, width), lambda b, h, i: (b, h)),
                  pl.BlockSpec((seq, width), lambda b, h, i: (b, h))],
        out_specs=pl.BlockSpec((SB_TQ, width), lambda b, h, i: (b * nq + i, h)),
        out_shape=jax.ShapeDtypeStruct((n_tok, SB_W), BF16),
        compiler_params=_params(("parallel", "parallel", "parallel")),
        name="sb_prompt",
    )(q, k, v)


def _sb_sample_kernel(q_ref, kn_ref, vn_ref, kc_ref, vc_ref, o_ref, *, heads, t_new, past):
    state = _sb_visit(heads, q_ref, lambda cols: kn_ref[:, cols].astype(BF16),
                      lambda cols: vn_ref[:, cols].astype(BF16), _later_key_matrix(t_new),
                      _sb_init(heads, t_new), _strict_causal(t_new))
    later = _later_key_matrix(SB_TK)
    n_blocks = past // SB_TK

    def visit(jj, st):
        rows = pl.ds(pl.multiple_of((n_blocks - 1 - jj) * SB_TK, SB_TK), SB_TK)
        return _sb_visit(heads, q_ref, lambda cols: kc_ref[0, 0, cols.start // SB_HD, rows, :].astype(BF16),
                         lambda cols: vc_ref[0, 0, cols.start // SB_HD, rows, :].astype(BF16), later, st, None)

    state = _sb_older_blocks(n_blocks, visit, state)
    for c in range(heads):
        o_ref[:, c * SB_HD:(c + 1) * SB_HD] = state[c][1].astype(o_ref.dtype)


def _sb_sample(q, k_new, v_new, k_cache, v_cache, layer, t_new):
    _, batch, _, past, _ = k_cache.shape
    heads = SB_SAMPLE_HEADS_PER_STEP
    width = heads * SB_HD
    new_spec = pl.BlockSpec((t_new, width), lambda b, h: (b, h))
    cache_spec = pl.BlockSpec((1, 1, heads, past, SB_HD), lambda b, h: (layer, b, h, 0, 0))
    return pl.pallas_call(
        functools.partial(_sb_sample_kernel, heads=heads, t_new=t_new, past=past),
        grid=(batch, SB_HEADS // heads),
        in_specs=[new_spec, new_spec, new_spec, cache_spec, cache_spec],
        out_specs=new_spec,
        out_shape=jax.ShapeDtypeStruct((batch * t_new, SB_W), BF16),
        compiler_params=_params(("parallel", "parallel")),
        name="sb_sample",
    )(q, k_new, v_new, k_cache, v_cache)


SWA_SCALE = SWA_HD ** -0.5
SWA_PAIRS = SWA_GROUP // 2
SWA_PAIR_W = 2 * SWA_HD
SWA_COLS = SWA_PAIRS * CHUNK
SWA_BIAS_VARIANTS = 3


def _swa_fill(dst_ref, front_ref, body_ref, seq):
    for g in range(SWA_KV_HEADS):
        for (lo, hi), src_ref in (((0, WINDOW), front_ref), ((WINDOW, WINDOW + seq), body_ref)):
            x = src_ref[:, g * SWA_HD:(g + 1) * SWA_HD]
            zero = jnp.zeros_like(x)
            dst_ref[g, 0, lo:hi, :] = jnp.concatenate([x, zero], axis=1).astype(BF16)
            dst_ref[g, 1, lo:hi, :] = jnp.concatenate([zero, x], axis=1).astype(BF16)


def _swa_softmax(s, sk):
    m = jnp.maximum(jnp.max(s, axis=0, keepdims=True), sk)
    p = jnp.exp(s - m)
    den = jnp.sum(p, axis=0, keepdims=True) + jnp.exp(sk - m)
    return p * (1.0 / den)


def _swa_kernel(q_ref, k_ref, v_ref, kf_ref, vf_ref, bias_ref, sink_ref, o_ref, kab_ref, vab_ref,
                *, chunks, seq, front_valid):
    t = pl.program_id(1)

    @pl.when(t == 0)
    def _():
        _swa_fill(kab_ref, kf_ref, k_ref, seq)
        _swa_fill(vab_ref, vf_ref, v_ref, seq)

    blocks = [(c, g) for c in range(chunks) for g in range(SWA_KV_HEADS)]

    def band(ref, c, g):
        rows = pl.ds(pl.multiple_of((t * chunks + c) * CHUNK, CHUNK), BAND)
        return jnp.concatenate([ref[g, 0, rows, :], ref[g, 1, rows, :]], axis=0)

    def q_pairs(c, g):
        return jnp.concatenate(
            [q_ref[c * CHUNK:(c + 1) * CHUNK, (g * SWA_PAIRS + r) * SWA_PAIR_W:(g * SWA_PAIRS + r + 1) * SWA_PAIR_W]
             for r in range(SWA_PAIRS)], axis=0)

    def variant(c):
        return SWA_BIAS_VARIANTS - 1 if front_valid else jnp.minimum(t * chunks + c, SWA_BIAS_VARIANTS - 1)

    s = [_dot_nt(band(kab_ref, c, g), q_pairs(c, g)) * SWA_SCALE + bias_ref[variant(c), g] for c, g in blocks]
    p = [jnp.concatenate([_swa_softmax(x[:BAND], sink_ref[g, 0]), _swa_softmax(x[BAND:], sink_ref[g, 1])],
                         axis=0).astype(BF16) for x, (c, g) in zip(s, blocks)]
    o = [lax.dot_general(band(vab_ref, c, g), x, (((0,), (0,)), ((), ())), preferred_element_type=F32).T
         for x, (c, g) in zip(p, blocks)]
    for x, (c, g) in zip(o, blocks):
        for r in range(SWA_PAIRS):
            lanes = slice((g * SWA_PAIRS + r) * SWA_PAIR_W, (g * SWA_PAIRS + r + 1) * SWA_PAIR_W)
            o_ref[c * CHUNK:(c + 1) * CHUNK, lanes] = x[r * CHUNK:(r + 1) * CHUNK].astype(o_ref.dtype)


def _swa(q, k, v, k_front, v_front, bias, sink, batch, seq, front_valid):
    chunks = min(4, seq // CHUNK)
    tile = chunks * CHUNK
    nt = seq // tile
    kv_spec = pl.BlockSpec((seq, SWA_KV_W), lambda b, t: (b, 0))
    front_spec = pl.BlockSpec((WINDOW, SWA_KV_W), lambda b, t: (b, 0))
    return pl.pallas_call(
        functools.partial(_swa_kernel, chunks=chunks, seq=seq, front_valid=front_valid),
        grid=(batch, nt),
        in_specs=[pl.BlockSpec((tile, SWA_W), lambda b, t: (b * nt + t, 0)),
                  kv_spec, kv_spec, front_spec, front_spec,
                  _resident(bias.shape), _resident(sink.shape)],
        out_specs=pl.BlockSpec((tile, SWA_W), lambda b, t: (b * nt + t, 0)),
        out_shape=jax.ShapeDtypeStruct((batch * seq, SWA_W), BF16),
        scratch_shapes=[pltpu.VMEM((SWA_KV_HEADS, 2, seq + WINDOW, SWA_PAIR_W), BF16)] * 2,
        compiler_params=_params(("parallel", "arbitrary")),
        name="swa",
    )(q, k, v, k_front, v_front, bias, sink)


def _t5_bucket(rel):
    nb = NUM_BUCKETS // 2
    max_exact = nb // 2
    ret = np.where(rel > 0, nb, 0)
    n = np.abs(rel)
    nf = np.maximum(n, 1).astype(np.float32)
    large = max_exact + (np.log(nf / np.float32(max_exact)) / np.float32(math.log(MAX_DISTANCE / max_exact))
                         * np.float32(nb - max_exact)).astype(np.int32)
    large = np.minimum(large, nb - 1)
    return ret + np.where(n < max_exact, n, large)


def _band_bias(rel_bias):
    rel = np.arange(BAND)[None, :] - WINDOW - np.arange(CHUNK)[:, None]
    b = jnp.take(rel_bias.astype(F32), jnp.asarray(_t5_bucket(rel), jnp.int32), axis=0)
    b = b.reshape(CHUNK, BAND, SWA_KV_HEADS, SWA_PAIRS, 2)
    b = jnp.transpose(b, (2, 4, 1, 3, 0)).reshape(SWA_KV_HEADS, 2 * BAND, SWA_COLS)
    first_valid = np.array([WINDOW - v * CHUNK for v in range(SWA_BIAS_VARIANTS)])
    key = np.tile(np.arange(BAND), 2)
    mask = np.where(key[None, :] < first_valid[:, None], NEG_INF, 0.0).astype(np.float32)
    return b[None] + jnp.asarray(mask)[:, None, :, None]


def _sink_cols(sink):
    s = jnp.transpose(sink.astype(F32).reshape(SWA_KV_HEADS, SWA_PAIRS, 2), (0, 2, 1))
    return jnp.repeat(s, CHUNK, axis=-1).reshape(SWA_KV_HEADS, 2, 1, SWA_COLS)


MEM_SCALE = MEM_HD ** -0.5


def _mem_attn_kernel(q_ref, k_ref, v_ref, o_ref):
    cols = [slice(h * MEM_HD, (h + 1) * MEM_HD) for h in range(MEM_HEADS)]
    ones = jnp.ones((MEM_LEN, MEM_HD), BF16)
    s = [_dot_nt(q_ref[:, cs], k_ref[:, cs].astype(BF16)) * MEM_SCALE for cs in cols]
    p = [jnp.exp(x - jnp.max(x, axis=-1, keepdims=True)).astype(BF16) for x in s]
    pv = [_dot(x, jnp.concatenate([v_ref[:, cs].astype(BF16), ones], axis=1)) for x, cs in zip(p, cols)]
    for x, cs in zip(pv, cols):
        o_ref[:, cs] = (x[:, :MEM_HD] * (1.0 / x[:, MEM_HD:])).astype(o_ref.dtype)


def _mem_attn(qm, mk, mv, batch, seq, tq):
    nt = seq // tq
    kv_spec = pl.BlockSpec((MEM_LEN, MEM_W), lambda b, t: (b, 0))
    return pl.pallas_call(
        _mem_attn_kernel,
        grid=(batch, nt),
        in_specs=[pl.BlockSpec((tq, MEM_W), lambda b, t: (b * nt + t, 0)), kv_spec, kv_spec],
        out_specs=pl.BlockSpec((tq, MEM_W), lambda b, t: (b * nt + t, 0)),
        out_shape=jax.ShapeDtypeStruct((batch * seq, MEM_W), BF16),
        compiler_params=_params(("parallel", "parallel")),
        name="mem_attn",
    )(qm, mk, mv)


OUT_PROJ_SUBTILES = 2


def _out_proj_kernel(o_ref, om_ref, gate_ref, x_ref, w_ref, g_ref, y_ref, *, mix_w):
    rows = o_ref.shape[0] // OUT_PROJ_SUBTILES
    subs = [slice(r * rows, (r + 1) * rows) for r in range(OUT_PROJ_SUBTILES)]
    y = [_dot(o_ref[rs, :] * gate_ref[rs, :mix_w], w_ref[:mix_w, :])
         + _dot(om_ref[rs, :] * gate_ref[rs, mix_w:], w_ref[mix_w:, :]) for rs in subs]
    for rs, yy in zip(subs, y):
        y_ref[rs, :] = x_ref[rs, :] + _rmsnorm(yy, g_ref[...])


def _out_proj(o, om, gate, x, w, gain, tm):
    n_tok, d = x.shape
    mix_w = o.shape[1]
    row = lambda width: pl.BlockSpec((tm, width), lambda i: (i, 0))
    return pl.pallas_call(
        functools.partial(_out_proj_kernel, mix_w=mix_w),
        grid=(n_tok // tm,),
        in_specs=[row(mix_w), row(MEM_W), row(mix_w + MEM_W), row(d),
                  _resident(w.shape), _resident((1, d))],
        out_specs=row(d),
        out_shape=jax.ShapeDtypeStruct((n_tok, d), F32),
        compiler_params=_params(("parallel",)),
        name="out_proj",
    )(o, om, gate, x, w, gain.reshape(1, d))


def _split_cols(w, widths):
    out, off = [], 0
    for width in widths:
        out.append(w[:, off:off + width].astype(BF16))
        off += width
    return out


@jax.jit
def kernel(x_prompt, x_sample, mem_prompt, cache_sb_k, cache_sb_v, cache_swa_k, cache_swa_v,
           cache_mem_k, cache_mem_v, pre_norm, post_norm, mem_norm, w_in_a, w_in_b, w_mem_kv,
           w_out, rel_bias, sinks):
    batch, seq, d = x_prompt.shape
    dec_batch, t_new, _ = x_sample.shape
    depth = pre_norm.shape[0]
    past = cache_sb_k.shape[2]
    n_p, n_s = batch * seq, dec_batch * t_new
    d_inner = w_out.shape[1]

    xp = x_prompt.reshape(n_p, d)
    xs = x_sample.reshape(n_s, d)
    proj_dtypes = (BF16, F32, F32, BF16, BF16)
    gate = (4,)
    widths_a = (SB_W, SB_W, SB_W, MEM_W, d_inner)
    widths_b = (SWA_W, SWA_KV_W, SWA_KV_W, MEM_W, d_inner)

    mem_k, mem_v = _mem_proj(mem_prompt.reshape(batch * MEM_LEN, d), mem_norm,
                             w_mem_kv.astype(BF16), tm=512)
    bias = _band_bias(rel_bias)
    zero_front = jnp.zeros((batch * WINDOW, SWA_KV_W), F32)

    n_a, n_b = (depth + 1) // 2, depth // 2
    sb_state_p = None
    heads_major = (0, 1, 3, 2, 4)
    cache_k = jnp.transpose(cache_sb_k, heads_major)
    cache_v = jnp.transpose(cache_sb_v, heads_major)
    sb_ks, sb_vs = [], []
    swa_kp, swa_vp, swa_ks, swa_vs = [], [], [], []
    for i in range(depth):
        j = i // 2
        if i % 2 == 0:
            weights = _split_cols(w_in_a[j], widths_a)
            (q_p, k_p, v_p, qm_p, z_p), sb_state_p = _norm_proj(
                xp, pre_norm[i], weights, (BF16, BF16, BF16, BF16, BF16), tm=512, gate_idx=gate,
                state_idx=(1, 2), slab=j, n_slabs=n_a, seq=seq, prev_states=sb_state_p)
            (q_s, k_s, v_s, qm_s, z_s), _ = _norm_proj(xs, pre_norm[i], weights, proj_dtypes, tm=256,
                                                       gate_idx=gate)
            o_p = _sb_prompt(q_p, k_p, v_p, batch, seq)
            o_s = _sb_sample(q_s, k_s, v_s, cache_k, cache_v, j, t_new)
            sb_ks.append(k_s)
            sb_vs.append(v_s)
        else:
            weights = _split_cols(w_in_b[j], widths_b)
            (q_p, k_p, v_p, qm_p, z_p), _ = _norm_proj(xp, pre_norm[i], weights, proj_dtypes, tm=512,
                                                       gate_idx=gate)
            (q_s, k_s, v_s, qm_s, z_s), _ = _norm_proj(xs, pre_norm[i], weights, proj_dtypes, tm=256,
                                                       gate_idx=gate)
            sink = _sink_cols(sinks[j])
            o_p = _swa(q_p, k_p, v_p, zero_front, zero_front, bias, sink, batch, seq, front_valid=False)
            kc = cache_swa_k[j].reshape(dec_batch * WINDOW, SWA_KV_W)
            vc = cache_swa_v[j].reshape(dec_batch * WINDOW, SWA_KV_W)
            o_s = _swa(q_s, k_s, v_s, kc, vc, bias, sink, dec_batch, t_new, front_valid=True)
            swa_kp.append(k_p.reshape(batch, seq, SWA_KV_W)[:, seq - WINDOW:])
            swa_vp.append(v_p.reshape(batch, seq, SWA_KV_W)[:, seq - WINDOW:])
            k_all = jnp.concatenate([kc.reshape(dec_batch, WINDOW, SWA_KV_W),
                                     k_s.reshape(dec_batch, t_new, SWA_KV_W)], axis=1)
            v_all = jnp.concatenate([vc.reshape(dec_batch, WINDOW, SWA_KV_W),
                                     v_s.reshape(dec_batch, t_new, SWA_KV_W)], axis=1)
            swa_ks.append(k_all[:, t_new:])
            swa_vs.append(v_all[:, t_new:])
        om_p = _mem_attn(qm_p, mem_k[i], mem_v[i], batch, seq, tq=512)
        om_s = _mem_attn(qm_s, cache_mem_k[i].reshape(dec_batch * MEM_LEN, MEM_W),
                         cache_mem_v[i].reshape(dec_batch * MEM_LEN, MEM_W), dec_batch, t_new, tq=t_new)
        w_o = w_out[i].astype(BF16)
        xp = _out_proj(o_p, om_p, z_p, xp, w_o, post_norm[i], tm=1024)
        xs = _out_proj(o_s, om_s, z_s, xs, w_o, post_norm[i], tm=256)

    return (xp.reshape(batch, seq, d), xs.reshape(dec_batch, t_new, d),
            jnp.transpose(sb_state_p[0], heads_major),
            jnp.transpose(sb_state_p[1], heads_major),
            jnp.stack(sb_ks).reshape(n_a, dec_batch, t_new, SB_HEADS, SB_HD),
            jnp.stack(sb_vs).reshape(n_a, dec_batch, t_new, SB_HEADS, SB_HD),
            jnp.stack(swa_kp).reshape(n_b, batch, WINDOW, SWA_KV_HEADS, SWA_HD),
            jnp.stack(swa_vp).reshape(n_b, batch, WINDOW, SWA_KV_HEADS, SWA_HD),
            jnp.stack(swa_ks).reshape(n_b, dec_batch, WINDOW, SWA_KV_HEADS, SWA_HD),
            jnp.stack(swa_vs).reshape(n_b, dec_batch, WINDOW, SWA_KV_HEADS, SWA_HD),
            mem_k.reshape(depth, batch, MEM_LEN, MEM_HEADS, MEM_HD),
            mem_v.reshape(depth, batch, MEM_LEN, MEM_HEADS, MEM_HD))
```

```python
import functools
import math

import numpy as np
import jax
import jax.numpy as jnp
from jax import lax
from jax.experimental import pallas as pl
from jax.experimental.pallas import tpu as pltpu

F32 = jnp.float32
BF16 = jnp.bfloat16

D_MODEL = 1024
CHUNK = 64
MEM_LEN = 256
MEM_HEADS = 4
MEM_HD = 128
MEM_W = MEM_HEADS * MEM_HD
SB_HEADS = 12
SB_HD = 128
SB_W = SB_HEADS * SB_HD
SWA_HEADS = 24
SWA_KV_HEADS = 3
SWA_GROUP = SWA_HEADS // SWA_KV_HEADS
SWA_HD = 64
SWA_W = SWA_HEADS * SWA_HD
SWA_KV_W = SWA_KV_HEADS * SWA_HD
WINDOW = 128
BAND = WINDOW + CHUNK
NUM_BUCKETS = 32
MAX_DISTANCE = 128
EPS = 1e-6
NEG_INF = -1e30

VMEM_LIMIT_BYTES = 56 * 1024 * 1024


def _params(semantics):
    return pltpu.CompilerParams(dimension_semantics=semantics,
                                vmem_limit_bytes=VMEM_LIMIT_BYTES)


def _resident(shape):
    nd = len(shape)
    return pl.BlockSpec(shape, lambda *_: (0,) * nd, pipeline_mode=pl.Buffered(1))


def _rmsnorm(x, g):
    return x * lax.rsqrt(jnp.mean(x * x, axis=-1, keepdims=True) + EPS) * g


def _dot(a, b):
    return jnp.dot(a, b, preferred_element_type=F32)


def _dot_nt(a, b):
    return lax.dot_general(a, b, (((1,), (1,)), ((), ())), preferred_element_type=F32)


PROJ_COL_CHUNK = 512


def _silu(z):
    return z * (1.0 / (1.0 + jnp.exp(-z)))


def _norm_proj_kernel(x_ref, g_ref, *refs, widths, state_idx, gate_idx, n_prev, slab):
    n = len(widths)
    w_refs = refs[:n]
    o_refs = refs[n + n_prev:2 * n + n_prev]
    s_refs = refs[2 * n + n_prev:]
    s_slab = 0 if n_prev else slab
    h = _rmsnorm(x_ref[...], g_ref[...]).astype(BF16)
    for idx, (w_ref, o_ref, width) in enumerate(zip(w_refs, o_refs, widths)):
        s_ref = s_refs[state_idx.index(idx)] if idx in state_idx else None
        for c in range(0, width, PROJ_COL_CHUNK):
            cw = min(PROJ_COL_CHUNK, width - c)
            y = _dot(h, w_ref[:, c:c + cw])
            o_ref[:, c:c + cw] = (_silu(y) if idx in gate_idx else y).astype(o_ref.dtype)
            if s_ref is not None:
                for hh in range(cw // SB_HD):
                    s_ref[s_slab, 0, c // SB_HD + hh, :, :] = y[:, hh * SB_HD:(hh + 1) * SB_HD]
    for s_ref in s_refs:
        for other in range(s_ref.shape[0]):
            if other != s_slab:
                s_ref[other] = jnp.zeros(s_ref.shape[1:], s_ref.dtype)


def _norm_proj(x, gain, weights, dtypes, tm, gate_idx=(), state_idx=(), slab=0, n_slabs=1, seq=None,
               prev_states=None):
    n_tok, d = x.shape
    widths = tuple(w.shape[1] for w in weights)
    prev = tuple(prev_states) if prev_states is not None else ()
    n_in = 2 + len(weights)
    state_specs, state_shapes = [], []
    if state_idx:
        nt = seq // tm
        if prev:
            spec = pl.BlockSpec((1, 1, SB_HEADS, tm, SB_HD), lambda i: (slab, i // nt, 0, i % nt, 0))
        else:
            spec = pl.BlockSpec((n_slabs, 1, SB_HEADS, tm, SB_HD), lambda i: (0, i // nt, 0, i % nt, 0))
        state_specs = [spec for _ in state_idx]
        state_shapes = [jax.ShapeDtypeStruct((n_slabs, n_tok // seq, SB_HEADS, seq, SB_HD), F32)
                        for _ in state_idx]
    outs = pl.pallas_call(
        functools.partial(_norm_proj_kernel, widths=widths, state_idx=tuple(state_idx),
                          gate_idx=tuple(gate_idx), n_prev=len(prev), slab=slab),
        grid=(n_tok // tm,),
        in_specs=[pl.BlockSpec((tm, d), lambda i: (i, 0)), _resident((1, d))]
        + [_resident(w.shape) for w in weights]
        + [pl.BlockSpec(memory_space=pl.ANY) for _ in prev],
        out_specs=[pl.BlockSpec((tm, w), lambda i: (i, 0)) for w in widths] + state_specs,
        out_shape=[jax.ShapeDtypeStruct((n_tok, w), dt) for w, dt in zip(widths, dtypes)] + state_shapes,
        input_output_aliases={n_in + s: len(widths) + s for s in range(len(prev))},
        compiler_params=_params(("parallel",)),
        name="norm_proj",
    )(x, gain.reshape(1, d), *weights, *prev)
    return outs[:len(widths)], outs[len(widths):]


def _mem_proj_kernel(x_ref, g_ref, w_ref, k_ref, v_ref):
    h = _rmsnorm(x_ref[...], g_ref[0]).astype(BF16)
    k_ref[0] = _dot(h, w_ref[0, :, :MEM_W])
    v_ref[0] = _dot(h, w_ref[0, :, MEM_W:])


def _mem_proj(mem, gains, w, tm):
    n_tok, d = mem.shape
    depth = w.shape[0]
    out = jax.ShapeDtypeStruct((depth, n_tok, MEM_W), F32)
    return pl.pallas_call(
        _mem_proj_kernel,
        grid=(depth, n_tok // tm),
        in_specs=[pl.BlockSpec((tm, d), lambda l, i: (i, 0)),
                  pl.BlockSpec((1, 1, d), lambda l, i: (l, 0, 0)),
                  pl.BlockSpec((1, d, 2 * MEM_W), lambda l, i: (l, 0, 0))],
        out_specs=[pl.BlockSpec((1, tm, MEM_W), lambda l, i: (l, i, 0))] * 2,
        out_shape=[out, out],
        compiler_params=_params(("parallel", "parallel")),
        name="mem_proj",
    )(mem, gains.reshape(depth, 1, d), w)


LOG2E = math.log2(math.e)
SB_LOG2_SCALE = SB_HD ** -0.5 * LOG2E
SB_TQ = 256
SB_TK = 256
SB_HEADS_PER_STEP = 6
SB_SAMPLE_HEADS_PER_STEP = 3
SB_LIVE_ROWS = 160
SB_DEAD_LOG2 = 151.0


def _later_key_matrix(n):
    later = lax.broadcasted_iota(jnp.int32, (n, n), 0) > lax.broadcasted_iota(jnp.int32, (n, n), 1)
    return jnp.where(later, 1.0, 0.0).astype(BF16)


def _strict_causal(n_rows, n_cols, first_row):
    return (lax.broadcasted_iota(jnp.int32, (n_rows, n_cols), 1)
            < lax.broadcasted_iota(jnp.int32, (n_rows, n_cols), 0) + first_row)


def _sb_visit(chains):
    z2 = [_dot_nt(c[0], c[1]) * SB_LOG2_SCALE for c in chains]
    nl = [jnp.maximum(z, 0.0) + jnp.log2(1.0 + jnp.exp2(-jnp.abs(z))) for z in z2]
    nl = [x if c[4] is None else jnp.where(c[4], x, 0.0) for x, c in zip(nl, chains)]
    ncarry = [c[5] + jnp.sum(x, axis=-1, keepdims=True) for x, c in zip(nl, chains)]
    head = [z - x - c[5] for z, x, c in zip(z2, nl, chains)]
    a = [jnp.exp2(t - _dot(x.astype(BF16), c[3])) for t, x, c in zip(head, nl, chains)]
    a = [x if c[4] is None else jnp.where(c[4], x, 0.0) for x, c in zip(a, chains)]
    acc = [c[6] + _dot(x.astype(BF16), c[2]) for x, c in zip(a, chains)]
    return tuple(zip(ncarry, acc))


def _sb_older_blocks(n_blocks, visit, nc_ref, acc_ref, live_rows=None):
    heads, total = nc_ref.shape[0], nc_ref.shape[1]

    def liveness():
        least = functools.reduce(jnp.minimum, [nc_ref[c] for c in range(heads)])
        if live_rows is None:
            return (jnp.min(least) < SB_DEAD_LOG2).astype(jnp.int32), jnp.int32(0)
        lead, tail = jnp.min(least[:live_rows]), jnp.min(least[live_rows:])
        return ((jnp.minimum(lead, tail) < SB_DEAD_LOG2).astype(jnp.int32),
                (tail >= SB_DEAD_LOG2).astype(jnp.int32))

    def step(block, rows):
        new = visit(block, rows, tuple((nc_ref[c, :rows], acc_ref[c, :rows]) for c in range(heads)))
        _sb_store(nc_ref, acc_ref, new, rows)

    def cond(loop):
        return jnp.logical_and(loop[0] < n_blocks, loop[1] > 0)

    def body(loop):
        block, _, tail_dead = loop
        if live_rows is None:
            step(block, total)
        else:
            pl.when(tail_dead > 0)(lambda: step(block, live_rows))
            pl.when(tail_dead == 0)(lambda: step(block, total))
        return (block + 1,) + liveness()

    lax.while_loop(cond, body, (jnp.int32(0),) + liveness())


def _sb_store(nc_ref, acc_ref, state, rows):
    for c, (ncarry, acc) in enumerate(state):
        nc_ref[c, :rows] = ncarry
        acc_ref[c, :rows] = acc


def _sb_zero(rows):
    return jnp.zeros((rows, 1), F32), jnp.zeros((rows, SB_HD), F32)


def _sb_prompt_kernel(q_ref, k_ref, v_ref, o_ref, nc_ref, acc_ref, *, heads):
    i = pl.program_id(2)
    cols = [slice(c * SB_HD, (c + 1) * SB_HD) for c in range(heads)]
    later = _later_key_matrix(SB_TK)
    diag = pl.ds(pl.multiple_of(i * SB_TK, SB_TK), SB_TK)
    mask = _strict_causal(SB_TQ, SB_TK, 0)
    _sb_store(nc_ref, acc_ref,
              _sb_visit([(q_ref[:, cs], k_ref[diag, cs], v_ref[diag, cs], later, mask) + _sb_zero(SB_TQ)
                         for cs in cols]), SB_TQ)

    def older(jj, rows, st):
        keys = pl.ds(pl.multiple_of((i - 1 - jj) * SB_TK, SB_TK), SB_TK)
        return _sb_visit([(q_ref[:rows, cs], k_ref[keys, cs], v_ref[keys, cs], later, None) + tuple(s)
                          for cs, s in zip(cols, st)])

    _sb_older_blocks(i, older, nc_ref, acc_ref, live_rows=SB_LIVE_ROWS)
    for c, cs in enumerate(cols):
        o_ref[:, cs] = acc_ref[c].astype(o_ref.dtype)


def _sb_prompt(q, k, v, batch, seq):
    n_tok = batch * seq
    nq = seq // SB_TQ
    heads = SB_HEADS_PER_STEP
    width = heads * SB_HD
    return pl.pallas_call(
        functools.partial(_sb_prompt_kernel, heads=heads),
        grid=(batch, SB_HEADS // heads, nq),
        in_specs=[pl.BlockSpec((SB_TQ, width), lambda b, h, i: (b * nq + i, h)),
                  pl.BlockSpec((seq, width), lambda b, h, i: (b, h)),
                  pl.BlockSpec((seq, width), lambda b, h, i: (b, h))],
        out_specs=pl.BlockSpec((SB_TQ, width), lambda b, h, i: (b * nq + i, h)),
        out_shape=jax.ShapeDtypeStruct((n_tok, SB_W), BF16),
        scratch_shapes=[pltpu.VMEM((heads, SB_TQ, 1), F32), pltpu.VMEM((heads, SB_TQ, SB_HD), F32)],
        compiler_params=_params(("parallel", "parallel", "parallel")),
        name="sb_prompt",
    )(q, k, v)


def _sb_sample_kernel(q_ref, kn_ref, vn_ref, kc_ref, vc_ref, o_ref, nc_ref, acc_ref, *, heads, t_new, past):
    cols = [slice(c * SB_HD, (c + 1) * SB_HD) for c in range(heads)]
    later_new, mask_new = _later_key_matrix(t_new), _strict_causal(t_new, t_new, 0)
    _sb_store(nc_ref, acc_ref,
              _sb_visit([(q_ref[:, cs], kn_ref[:, cs].astype(BF16), vn_ref[:, cs].astype(BF16), later_new, mask_new)
                         + _sb_zero(t_new) for cs in cols]), t_new)
    later = _later_key_matrix(SB_TK)
    n_blocks = past // SB_TK

    def older(jj, rows, st):
        keys = pl.ds(pl.multiple_of((n_blocks - 1 - jj) * SB_TK, SB_TK), SB_TK)
        return _sb_visit([(q_ref[:, cs], kc_ref[0, 0, c, keys, :].astype(BF16), vc_ref[0, 0, c, keys, :].astype(BF16),
                           later, None) + tuple(s) for c, (cs, s) in enumerate(zip(cols, st))])

    _sb_older_blocks(n_blocks, older, nc_ref, acc_ref)
    for c, cs in enumerate(cols):
        o_ref[:, cs] = acc_ref[c].astype(o_ref.dtype)


def _sb_sample(q, k_new, v_new, k_cache, v_cache, layer, t_new):
    _, batch, _, past, _ = k_cache.shape
    heads = SB_SAMPLE_HEADS_PER_STEP
    width = heads * SB_HD
    new_spec = pl.BlockSpec((t_new, width), lambda b, h: (b, h))
    cache_spec = pl.BlockSpec((1, 1, heads, past, SB_HD), lambda b, h: (layer, b, h, 0, 0))
    return pl.pallas_call(
        functools.partial(_sb_sample_kernel, heads=heads, t_new=t_new, past=past),
        grid=(batch, SB_HEADS // heads),
        in_specs=[new_spec, new_spec, new_spec, cache_spec, cache_spec],
        out_specs=new_spec,
        out_shape=jax.ShapeDtypeStruct((batch * t_new, SB_W), BF16),
        scratch_shapes=[pltpu.VMEM((heads, t_new, 1), F32), pltpu.VMEM((heads, t_new, SB_HD), F32)],
        compiler_params=_params(("parallel", "parallel")),
        name="sb_sample",
    )(q, k_new, v_new, k_cache, v_cache)


SWA_SCALE = SWA_HD ** -0.5
SWA_PAIRS = SWA_GROUP // 2
SWA_PAIR_W = 2 * SWA_HD
SWA_COLS = SWA_PAIRS * CHUNK
SWA_BIAS_VARIANTS = 3


def _swa_fill(dst_ref, front_ref, body_ref, seq):
    for g in range(SWA_KV_HEADS):
        for (lo, hi), src_ref in (((0, WINDOW), front_ref), ((WINDOW, WINDOW + seq), body_ref)):
            x = src_ref[:, g * SWA_HD:(g + 1) * SWA_HD]
            zero = jnp.zeros_like(x)
            dst_ref[g, 0, lo:hi, :] = jnp.concatenate([x, zero], axis=1).astype(BF16)
            dst_ref[g, 1, lo:hi, :] = jnp.concatenate([zero, x], axis=1).astype(BF16)


def _swa_softmax(s, sk):
    m = jnp.maximum(jnp.max(s, axis=0, keepdims=True), sk)
    p = jnp.exp(s - m)
    den = jnp.sum(p, axis=0, keepdims=True) + jnp.exp(sk - m)
    return p * (1.0 / den)


def _swa_kernel(q_ref, k_ref, v_ref, kf_ref, vf_ref, bias_ref, sink_ref, o_ref, kab_ref, vab_ref,
                *, chunks, seq, front_valid):
    t = pl.program_id(1)

    @pl.when(t == 0)
    def _():
        _swa_fill(kab_ref, kf_ref, k_ref, seq)
        _swa_fill(vab_ref, vf_ref, v_ref, seq)

    blocks = [(c, g) for c in range(chunks) for g in range(SWA_KV_HEADS)]

    def band(ref, c, g):
        rows = pl.ds(pl.multiple_of((t * chunks + c) * CHUNK, CHUNK), BAND)
        return jnp.concatenate([ref[g, 0, rows, :], ref[g, 1, rows, :]], axis=0)

    def q_pairs(c, g):
        return jnp.concatenate(
            [q_ref[c * CHUNK:(c + 1) * CHUNK, (g * SWA_PAIRS + r) * SWA_PAIR_W:(g * SWA_PAIRS + r + 1) * SWA_PAIR_W]
             for r in range(SWA_PAIRS)], axis=0)

    def variant(c):
        return SWA_BIAS_VARIANTS - 1 if front_valid else jnp.minimum(t * chunks + c, SWA_BIAS_VARIANTS - 1)

    s = [_dot_nt(band(kab_ref, c, g), q_pairs(c, g)) * SWA_SCALE + bias_ref[variant(c), g] for c, g in blocks]
    p = [jnp.concatenate([_swa_softmax(x[:BAND], sink_ref[g, 0]), _swa_softmax(x[BAND:], sink_ref[g, 1])],
                         axis=0).astype(BF16) for x, (c, g) in zip(s, blocks)]
    o = [lax.dot_general(band(vab_ref, c, g), x, (((0,), (0,)), ((), ())), preferred_element_type=F32).T
         for x, (c, g) in zip(p, blocks)]
    for x, (c, g) in zip(o, blocks):
        for r in range(SWA_PAIRS):
            lanes = slice((g * SWA_PAIRS + r) * SWA_PAIR_W, (g * SWA_PAIRS + r + 1) * SWA_PAIR_W)
            o_ref[c * CHUNK:(c + 1) * CHUNK, lanes] = x[r * CHUNK:(r + 1) * CHUNK].astype(o_ref.dtype)


def _swa(q, k, v, k_front, v_front, bias, sink, batch, seq, front_valid):
    chunks = min(4, seq // CHUNK)
    tile = chunks * CHUNK
    nt = seq // tile
    kv_spec = pl.BlockSpec((seq, SWA_KV_W), lambda b, t: (b, 0))
    front_spec = pl.BlockSpec((WINDOW, SWA_KV_W), lambda b, t: (b, 0))
    return pl.pallas_call(
        functools.partial(_swa_kernel, chunks=chunks, seq=seq, front_valid=front_valid),
        grid=(batch, nt),
        in_specs=[pl.BlockSpec((tile, SWA_W), lambda b, t: (b * nt + t, 0)),
                  kv_spec, kv_spec, front_spec, front_spec,
                  _resident(bias.shape), _resident(sink.shape)],
        out_specs=pl.BlockSpec((tile, SWA_W), lambda b, t: (b * nt + t, 0)),
        out_shape=jax.ShapeDtypeStruct((batch * seq, SWA_W), BF16),
        scratch_shapes=[pltpu.VMEM((SWA_KV_HEADS, 2, seq + WINDOW, SWA_PAIR_W), BF16)] * 2,
        compiler_params=_params(("parallel", "arbitrary")),
        name="swa",
    )(q, k, v, k_front, v_front, bias, sink)


def _t5_bucket(rel):
    nb = NUM_BUCKETS // 2
    max_exact = nb // 2
    ret = np.where(rel > 0, nb, 0)
    n = np.abs(rel)
    nf = np.maximum(n, 1).astype(np.float32)
    large = max_exact + (np.log(nf / np.float32(max_exact)) / np.float32(math.log(MAX_DISTANCE / max_exact))
                         * np.float32(nb - max_exact)).astype(np.int32)
    large = np.minimum(large, nb - 1)
    return ret + np.where(n < max_exact, n, large)


def _band_bias(rel_bias):
    rel = np.arange(BAND)[None, :] - WINDOW - np.arange(CHUNK)[:, None]
    b = jnp.take(rel_bias.astype(F32), jnp.asarray(_t5_bucket(rel), jnp.int32), axis=0)
    b = b.reshape(CHUNK, BAND, SWA_KV_HEADS, SWA_PAIRS, 2)
    b = jnp.transpose(b, (2, 4, 1, 3, 0)).reshape(SWA_KV_HEADS, 2 * BAND, SWA_COLS)
    first_valid = np.array([WINDOW - v * CHUNK for v in range(SWA_BIAS_VARIANTS)])
    key = np.tile(np.arange(BAND), 2)
    mask = np.where(key[None, :] < first_valid[:, None], NEG_INF, 0.0).astype(np.float32)
    return b[None] + jnp.asarray(mask)[:, None, :, None]


def _sink_cols(sink):
    s = jnp.transpose(sink.astype(F32).reshape(SWA_KV_HEADS, SWA_PAIRS, 2), (0, 2, 1))
    return jnp.repeat(s, CHUNK, axis=-1).reshape(SWA_KV_HEADS, 2, 1, SWA_COLS)


MEM_SCALE = MEM_HD ** -0.5


def _mem_attn_kernel(q_ref, k_ref, v_ref, o_ref):
    cols = [slice(h * MEM_HD, (h + 1) * MEM_HD) for h in range(MEM_HEADS)]
    ones = jnp.ones((MEM_LEN, MEM_HD), BF16)
    s = [_dot_nt(q_ref[:, cs], k_ref[:, cs].astype(BF16)) * MEM_SCALE for cs in cols]
    p = [jnp.exp(x - jnp.max(x, axis=-1, keepdims=True)).astype(BF16) for x in s]
    pv = [_dot(x, jnp.concatenate([v_ref[:, cs].astype(BF16), ones], axis=1)) for x, cs in zip(p, cols)]
    for x, cs in zip(pv, cols):
        o_ref[:, cs] = (x[:, :MEM_HD] * (1.0 / x[:, MEM_HD:])).astype(o_ref.dtype)


def _mem_attn(qm, mk, mv, batch, seq, tq):
    nt = seq // tq
    kv_spec = pl.BlockSpec((MEM_LEN, MEM_W), lambda b, t: (b, 0))
    return pl.pallas_call(
        _mem_attn_kernel,
        grid=(batch, nt),
        in_specs=[pl.BlockSpec((tq, MEM_W), lambda b, t: (b * nt + t, 0)), kv_spec, kv_spec],
        out_specs=pl.BlockSpec((tq, MEM_W), lambda b, t: (b * nt + t, 0)),
        out_shape=jax.ShapeDtypeStruct((batch * seq, MEM_W), BF16),
        compiler_params=_params(("parallel", "parallel")),
        name="mem_attn",
    )(qm, mk, mv)


OUT_PROJ_SUBTILES = 2


def _out_proj_kernel(o_ref, om_ref, gate_ref, x_ref, w_ref, g_ref, y_ref, *, mix_w):
    rows = o_ref.shape[0] // OUT_PROJ_SUBTILES
    subs = [slice(r * rows, (r + 1) * rows) for r in range(OUT_PROJ_SUBTILES)]
    y = [_dot(o_ref[rs, :] * gate_ref[rs, :mix_w], w_ref[:mix_w, :])
         + _dot(om_ref[rs, :] * gate_ref[rs, mix_w:], w_ref[mix_w:, :]) for rs in subs]
    for rs, yy in zip(subs, y):
        y_ref[rs, :] = x_ref[rs, :] + _rmsnorm(yy, g_ref[...])


def _out_proj(o, om, gate, x, w, gain, tm):
    n_tok, d = x.shape
    mix_w = o.shape[1]
    row = lambda width: pl.BlockSpec((tm, width), lambda i: (i, 0))
    return pl.pallas_call(
        functools.partial(_out_proj_kernel, mix_w=mix_w),
        grid=(n_tok // tm,),
        in_specs=[row(mix_w), row(MEM_W), row(mix_w + MEM_W), row(d),
                  _resident(w.shape), _resident((1, d))],
        out_specs=row(d),
        out_shape=jax.ShapeDtypeStruct((n_tok, d), F32),
        compiler_params=_params(("parallel",)),
        name="out_proj",
    )(o, om, gate, x, w, gain.reshape(1, d))


def _split_cols(w, widths):
    out, off = [], 0
    for width in widths:
        out.append(w[:, off:off + width].astype(BF16))
        off += width
    return out


@jax.jit
def kernel(x_prompt, x_sample, mem_prompt, cache_sb_k, cache_sb_v, cache_swa_k, cache_swa_v,
           cache_mem_k, cache_mem_v, pre_norm, post_norm, mem_norm, w_in_a, w_in_b, w_mem_kv,
           w_out, rel_bias, sinks):
    batch, seq, d = x_prompt.shape
    dec_batch, t_new, _ = x_sample.shape
    depth = pre_norm.shape[0]
    past = cache_sb_k.shape[2]
    n_p, n_s = batch * seq, dec_batch * t_new
    d_inner = w_out.shape[1]

    xp = x_prompt.reshape(n_p, d)
    xs = x_sample.reshape(n_s, d)
    proj_dtypes = (BF16, F32, F32, BF16, BF16)
    gate = (4,)
    widths_a = (SB_W, SB_W, SB_W, MEM_W, d_inner)
    widths_b = (SWA_W, SWA_KV_W, SWA_KV_W, MEM_W, d_inner)

    mem_k, mem_v = _mem_proj(mem_prompt.reshape(batch * MEM_LEN, d), mem_norm,
                             w_mem_kv.astype(BF16), tm=512)
    bias = _band_bias(rel_bias)
    zero_front = jnp.zeros((batch * WINDOW, SWA_KV_W), F32)

    n_a, n_b = (depth + 1) // 2, depth // 2
    sb_state_p = None
    heads_major = (0, 1, 3, 2, 4)
    cache_k = jnp.transpose(cache_sb_k, heads_major)
    cache_v = jnp.transpose(cache_sb_v, heads_major)
    sb_ks, sb_vs = [], []
    swa_kp, swa_vp, swa_ks, swa_vs = [], [], [], []
    for i in range(depth):
        j = i // 2
        if i % 2 == 0:
            weights = _split_cols(w_in_a[j], widths_a)
            (q_p, k_p, v_p, qm_p, z_p), sb_state_p = _norm_proj(
                xp, pre_norm[i], weights, (BF16, BF16, BF16, BF16, BF16),
                tm=256 if sb_state_p is None else 512,
                gate_idx=gate,
                state_idx=(1, 2), slab=j, n_slabs=n_a, seq=seq, prev_states=sb_state_p)
            (q_s, k_s, v_s, qm_s, z_s), _ = _norm_proj(xs, pre_norm[i], weights, proj_dtypes, tm=256,
                                                       gate_idx=gate)
            o_p = _sb_prompt(q_p, k_p, v_p, batch, seq)
            o_s = _sb_sample(q_s, k_s, v_s, cache_k, cache_v, j, t_new)
            sb_ks.append(k_s)
            sb_vs.append(v_s)
        else:
            weights = _split_cols(w_in_b[j], widths_b)
            (q_p, k_p, v_p, qm_p, z_p), _ = _norm_proj(xp, pre_norm[i], weights, proj_dtypes, tm=512,
                                                       gate_idx=gate)
            (q_s, k_s, v_s, qm_s, z_s), _ = _norm_proj(xs, pre_norm[i], weights, proj_dtypes, tm=256,
                                                       gate_idx=gate)
            sink = _sink_cols(sinks[j])
            o_p = _swa(q_p, k_p, v_p, zero_front, zero_front, bias, sink, batch, seq, front_valid=False)
            kc = cache_swa_k[j].reshape(dec_batch * WINDOW, SWA_KV_W)
            vc = cache_swa_v[j].reshape(dec_batch * WINDOW, SWA_KV_W)
            o_s = _swa(q_s, k_s, v_s, kc, vc, bias, sink, dec_batch, t_new, front_valid=True)
            swa_kp.append(k_p.reshape(batch, seq, SWA_KV_W)[:, seq - WINDOW:])
            swa_vp.append(v_p.reshape(batch, seq, SWA_KV_W)[:, seq - WINDOW:])
            k_all = jnp.concatenate([kc.reshape(dec_batch, WINDOW, SWA_KV_W),
                                     k_s.reshape(dec_batch, t_new, SWA_KV_W)], axis=1)
            v_all = jnp.concatenate([vc.reshape(dec_batch, WINDOW, SWA_KV_W),
                                     v_s.reshape(dec_batch, t_new, SWA_KV_W)], axis=1)
            swa_ks.append(k_all[:, t_new:])
            swa_vs.append(v_all[:, t_new:])
        om_p = _mem_attn(qm_p, mem_k[i], mem_v[i], batch, seq, tq=512)
        om_s = _mem_attn(qm_s, cache_mem_k[i].reshape(dec_batch * MEM_LEN, MEM_W),
                         cache_mem_v[i].reshape(dec_batch * MEM_LEN, MEM_W), dec_batch, t_new, tq=t_new)
        w_o = w_out[i].astype(BF16)
        xp = _out_proj(o_p, om_p, z_p, xp, w_o, post_norm[i], tm=1024)
        xs = _out_proj(o_s, om_s, z_s, xs, w_o, post_norm[i], tm=256)

    return (xp.reshape(batch, seq, d), xs.reshape(dec_batch, t_new, d),
            jnp.transpose(sb_state_p[0], heads_major),
            jnp.transpose(sb_state_p[1], heads_major),
            jnp.stack(sb_ks).reshape(n_a, dec_batch, t_new, SB_HEADS, SB_HD),
            jnp.stack(sb_vs).reshape(n_a, dec_batch, t_new, SB_HEADS, SB_HD),
            jnp.stack(swa_kp).reshape(n_b, batch, WINDOW, SWA_KV_HEADS, SWA_HD),
            jnp.stack(swa_vp).reshape(n_b, batch, WINDOW, SWA_KV_HEADS, SWA_HD),
            jnp.stack(swa_ks).reshape(n_b, dec_batch, WINDOW, SWA_KV_HEADS, SWA_HD),
            jnp.stack(swa_vs).reshape(n_b, dec_batch, WINDOW, SWA_KV_HEADS, SWA_HD),
            mem_k.reshape(depth, batch, MEM_LEN, MEM_HEADS, MEM_HD),
            mem_v.reshape(depth, batch, MEM_LEN, MEM_HEADS, MEM_HD))
```

```python
import functools
import math

import numpy as np
import jax
import jax.numpy as jnp
from jax import lax
from jax.experimental import pallas as pl
from jax.experimental.pallas import tpu as pltpu

F32 = jnp.float32
BF16 = jnp.bfloat16

D_MODEL = 1024
CHUNK = 64
MEM_LEN = 256
MEM_HEADS = 4
MEM_HD = 128
MEM_W = MEM_HEADS * MEM_HD
SB_HEADS = 12
SB_HD = 128
SB_W = SB_HEADS * SB_HD
SWA_HEADS = 24
SWA_KV_HEADS = 3
SWA_GROUP = SWA_HEADS // SWA_KV_HEADS
SWA_HD = 64
SWA_W = SWA_HEADS * SWA_HD
SWA_KV_W = SWA_KV_HEADS * SWA_HD
WINDOW = 128
BAND = WINDOW + CHUNK
NUM_BUCKETS = 32
MAX_DISTANCE = 128
EPS = 1e-6
NEG_INF = -1e30

VMEM_LIMIT_BYTES = 56 * 1024 * 1024


def _params(semantics):
    return pltpu.CompilerParams(dimension_semantics=semantics,
                                vmem_limit_bytes=VMEM_LIMIT_BYTES)


def _resident(shape):
    nd = len(shape)
    return pl.BlockSpec(shape, lambda *_: (0,) * nd, pipeline_mode=pl.Buffered(1))


def _rmsnorm(x, g):
    return x * lax.rsqrt(jnp.mean(x * x, axis=-1, keepdims=True) + EPS) * g


def _dot(a, b):
    return jnp.dot(a, b, preferred_element_type=F32)


def _dot_nt(a, b):
    return lax.dot_general(a, b, (((1,), (1,)), ((), ())), preferred_element_type=F32)


PROJ_COL_CHUNK = 512
PROJ_SUBTILES = 2


def _silu(z):
    return z * (1.0 / (1.0 + jnp.exp(-z)))


def _norm_proj_kernel(x_ref, g_ref, *refs, widths, state_idx, gate_idx, n_prev, slab):
    n = len(widths)
    w_refs = refs[:n]
    o_refs = refs[n + n_prev:2 * n + n_prev]
    s_refs = refs[2 * n + n_prev:]
    s_slab = 0 if n_prev else slab
    rows = x_ref.shape[0] // PROJ_SUBTILES
    subs = [slice(r * rows, (r + 1) * rows) for r in range(PROJ_SUBTILES)]
    h = [_rmsnorm(x_ref[rs, :], g_ref[...]).astype(BF16) for rs in subs]
    for idx, (w_ref, o_ref, width) in enumerate(zip(w_refs, o_refs, widths)):
        s_ref = s_refs[state_idx.index(idx)] if idx in state_idx else None
        for c in range(0, width, PROJ_COL_CHUNK):
            cw = min(PROJ_COL_CHUNK, width - c)
            for rs, hs in zip(subs, h):
                y = _dot(hs, w_ref[:, c:c + cw])
                o_ref[rs, c:c + cw] = (_silu(y) if idx in gate_idx else y).astype(o_ref.dtype)
                if s_ref is not None:
                    for hh in range(cw // SB_HD):
                        s_ref[s_slab, 0, c // SB_HD + hh, rs, :] = y[:, hh * SB_HD:(hh + 1) * SB_HD]
    for s_ref in s_refs:
        for other in range(s_ref.shape[0]):
            if other != s_slab:
                s_ref[other] = jnp.zeros(s_ref.shape[1:], s_ref.dtype)


def _norm_proj(x, gain, weights, dtypes, tm, gate_idx=(), state_idx=(), slab=0, n_slabs=1, seq=None,
               prev_states=None):
    n_tok, d = x.shape
    widths = tuple(w.shape[1] for w in weights)
    prev = tuple(prev_states) if prev_states is not None else ()
    n_in = 2 + len(weights)
    state_specs, state_shapes = [], []
    if state_idx:
        nt = seq // tm
        if prev:
            spec = pl.BlockSpec((1, 1, SB_HEADS, tm, SB_HD), lambda i: (slab, i // nt, 0, i % nt, 0))
        else:
            spec = pl.BlockSpec((n_slabs, 1, SB_HEADS, tm, SB_HD), lambda i: (0, i // nt, 0, i % nt, 0))
        state_specs = [spec for _ in state_idx]
        state_shapes = [jax.ShapeDtypeStruct((n_slabs, n_tok // seq, SB_HEADS, seq, SB_HD), F32)
                        for _ in state_idx]
    outs = pl.pallas_call(
        functools.partial(_norm_proj_kernel, widths=widths, state_idx=tuple(state_idx),
                          gate_idx=tuple(gate_idx), n_prev=len(prev), slab=slab),
        grid=(n_tok // tm,),
        in_specs=[pl.BlockSpec((tm, d), lambda i: (i, 0)), _resident((1, d))]
        + [_resident(w.shape) for w in weights]
        + [pl.BlockSpec(memory_space=pl.ANY) for _ in prev],
        out_specs=[pl.BlockSpec((tm, w), lambda i: (i, 0)) for w in widths] + state_specs,
        out_shape=[jax.ShapeDtypeStruct((n_tok, w), dt) for w, dt in zip(widths, dtypes)] + state_shapes,
        input_output_aliases={n_in + s: len(widths) + s for s in range(len(prev))},
        compiler_params=_params(("parallel",)),
        name="norm_proj",
    )(x, gain.reshape(1, d), *weights, *prev)
    return outs[:len(widths)], outs[len(widths):]


def _mem_proj_kernel(x_ref, g_ref, w_ref, k_ref, v_ref):
    h = _rmsnorm(x_ref[...], g_ref[0]).astype(BF16)
    k_ref[0] = _dot(h, w_ref[0, :, :MEM_W])
    v_ref[0] = _dot(h, w_ref[0, :, MEM_W:])


def _mem_proj(mem, gains, w, tm):
    n_tok, d = mem.shape
    depth = w.shape[0]
    out = jax.ShapeDtypeStruct((depth, n_tok, MEM_W), F32)
    return pl.pallas_call(
        _mem_proj_kernel,
        grid=(depth, n_tok // tm),
        in_specs=[pl.BlockSpec((tm, d), lambda l, i: (i, 0)),
                  pl.BlockSpec((1, 1, d), lambda l, i: (l, 0, 0)),
                  pl.BlockSpec((1, d, 2 * MEM_W), lambda l, i: (l, 0, 0))],
        out_specs=[pl.BlockSpec((1, tm, MEM_W), lambda l, i: (l, i, 0))] * 2,
        out_shape=[out, out],
        compiler_params=_params(("parallel", "parallel")),
        name="mem_proj",
    )(mem, gains.reshape(depth, 1, d), w)


LOG2E = math.log2(math.e)
SB_LOG2_SCALE = SB_HD ** -0.5 * LOG2E
SB_TQ = 256
SB_TK = 256
SB_HEADS_PER_STEP = 6
SB_SAMPLE_HEADS_PER_STEP = 3
SB_LIVE_ROWS = 160
SB_DEAD_LOG2 = 151.0


def _later_key_matrix(n):
    later = lax.broadcasted_iota(jnp.int32, (n, n), 0) > lax.broadcasted_iota(jnp.int32, (n, n), 1)
    return jnp.where(later, 1.0, 0.0).astype(BF16)


def _strict_causal(n_rows, n_cols, first_row):
    return (lax.broadcasted_iota(jnp.int32, (n_rows, n_cols), 1)
            < lax.broadcasted_iota(jnp.int32, (n_rows, n_cols), 0) + first_row)


def _sb_visit(chains):
    z2 = [_dot_nt(c[0], c[1]) * SB_LOG2_SCALE for c in chains]
    nl = [jnp.maximum(z, 0.0) + jnp.log2(1.0 + jnp.exp2(-jnp.abs(z))) for z in z2]
    nl = [x if c[4] is None else jnp.where(c[4], x, 0.0) for x, c in zip(nl, chains)]
    ncarry = [c[5] + jnp.sum(x, axis=-1, keepdims=True) for x, c in zip(nl, chains)]
    head = [z - x - c[5] for z, x, c in zip(z2, nl, chains)]
    a = [jnp.exp2(t - _dot(x.astype(BF16), c[3])) for t, x, c in zip(head, nl, chains)]
    a = [x if c[4] is None else jnp.where(c[4], x, 0.0) for x, c in zip(a, chains)]
    acc = [c[6] + _dot(x.astype(BF16), c[2]) for x, c in zip(a, chains)]
    return tuple(zip(ncarry, acc))


def _sb_older_blocks(n_blocks, visit, nc_ref, acc_ref, live_rows=None):
    heads, total = nc_ref.shape[0], nc_ref.shape[1]

    def liveness():
        least = functools.reduce(jnp.minimum, [nc_ref[c] for c in range(heads)])
        if live_rows is None:
            return (jnp.min(least) < SB_DEAD_LOG2).astype(jnp.int32), jnp.int32(0)
        lead, tail = jnp.min(least[:live_rows]), jnp.min(least[live_rows:])
        return ((jnp.minimum(lead, tail) < SB_DEAD_LOG2).astype(jnp.int32),
                (tail >= SB_DEAD_LOG2).astype(jnp.int32))

    def step(block, rows):
        new = visit(block, rows, tuple((nc_ref[c, :rows], acc_ref[c, :rows]) for c in range(heads)))
        _sb_store(nc_ref, acc_ref, new, rows)

    def cond(loop):
        return jnp.logical_and(loop[0] < n_blocks, loop[1] > 0)

    def body(loop):
        block, _, tail_dead = loop
        if live_rows is None:
            step(block, total)
        else:
            pl.when(tail_dead > 0)(lambda: step(block, live_rows))
            pl.when(tail_dead == 0)(lambda: step(block, total))
        return (block + 1,) + liveness()

    lax.while_loop(cond, body, (jnp.int32(0),) + liveness())


def _sb_store(nc_ref, acc_ref, state, rows):
    for c, (ncarry, acc) in enumerate(state):
        nc_ref[c, :rows] = ncarry
        acc_ref[c, :rows] = acc


def _sb_zero(rows):
    return jnp.zeros((rows, 1), F32), jnp.zeros((rows, SB_HD), F32)


def _sb_prompt_kernel(q_ref, k_ref, v_ref, o_ref, nc_ref, acc_ref, *, heads):
    i = pl.program_id(2)
    cols = [slice(c * SB_HD, (c + 1) * SB_HD) for c in range(heads)]
    later = _later_key_matrix(SB_TK)
    diag = pl.ds(pl.multiple_of(i * SB_TK, SB_TK), SB_TK)
    mask = _strict_causal(SB_TQ, SB_TK, 0)
    _sb_store(nc_ref, acc_ref,
              _sb_visit([(q_ref[:, cs], k_ref[diag, cs], v_ref[diag, cs], later, mask) + _sb_zero(SB_TQ)
                         for cs in cols]), SB_TQ)

    def older(jj, rows, st):
        keys = pl.ds(pl.multiple_of((i - 1 - jj) * SB_TK, SB_TK), SB_TK)
        return _sb_visit([(q_ref[:rows, cs], k_ref[keys, cs], v_ref[keys, cs], later, None) + tuple(s)
                          for cs, s in zip(cols, st)])

    _sb_older_blocks(i, older, nc_ref, acc_ref, live_rows=SB_LIVE_ROWS)
    for c, cs in enumerate(cols):
        o_ref[:, cs] = acc_ref[c].astype(o_ref.dtype)


def _sb_prompt(q, k, v, batch, seq):
    n_tok = batch * seq
    nq = seq // SB_TQ
    heads = SB_HEADS_PER_STEP
    width = heads * SB_HD
    return pl.pallas_call(
        functools.partial(_sb_prompt_kernel, heads=heads),
        grid=(batch, SB_HEADS // heads, nq),
        in_specs=[pl.BlockSpec((SB_TQ, width), lambda b, h, i: (b * nq + i, h)),
                  pl.BlockSpec((seq, width), lambda b, h, i: (b, h)),
                  pl.BlockSpec((seq, width), lambda b, h, i: (b, h))],
        out_specs=pl.BlockSpec((SB_TQ, width), lambda b, h, i: (b * nq + i, h)),
        out_shape=jax.ShapeDtypeStruct((n_tok, SB_W), BF16),
        scratch_shapes=[pltpu.VMEM((heads, SB_TQ, 1), F32), pltpu.VMEM((heads, SB_TQ, SB_HD), F32)],
        compiler_params=_params(("parallel", "parallel", "parallel")),
        name="sb_prompt",
    )(q, k, v)


def _sb_sample_kernel(q_ref, kn_ref, vn_ref, kc_ref, vc_ref, o_ref, nc_ref, acc_ref, *, heads, t_new, past):
    cols = [slice(c * SB_HD, (c + 1) * SB_HD) for c in range(heads)]
    later_new, mask_new = _later_key_matrix(t_new), _strict_causal(t_new, t_new, 0)
    _sb_store(nc_ref, acc_ref,
              _sb_visit([(q_ref[:, cs], kn_ref[:, cs].astype(BF16), vn_ref[:, cs].astype(BF16), later_new, mask_new)
                         + _sb_zero(t_new) for cs in cols]), t_new)
    later = _later_key_matrix(SB_TK)
    n_blocks = past // SB_TK

    def older(jj, rows, st):
        keys = pl.ds(pl.multiple_of((n_blocks - 1 - jj) * SB_TK, SB_TK), SB_TK)
        return _sb_visit([(q_ref[:, cs], kc_ref[0, 0, c, keys, :].astype(BF16), vc_ref[0, 0, c, keys, :].astype(BF16),
                           later, None) + tuple(s) for c, (cs, s) in enumerate(zip(cols, st))])

    _sb_older_blocks(n_blocks, older, nc_ref, acc_ref)
    for c, cs in enumerate(cols):
        o_ref[:, cs] = acc_ref[c].astype(o_ref.dtype)


def _sb_sample(q, k_new, v_new, k_cache, v_cache, layer, t_new):
    _, batch, _, past, _ = k_cache.shape
    heads = SB_SAMPLE_HEADS_PER_STEP
    width = heads * SB_HD
    new_spec = pl.BlockSpec((t_new, width), lambda b, h: (b, h))
    cache_spec = pl.BlockSpec((1, 1, heads, past, SB_HD), lambda b, h: (layer, b, h, 0, 0))
    return pl.pallas_call(
        functools.partial(_sb_sample_kernel, heads=heads, t_new=t_new, past=past),
        grid=(batch, SB_HEADS // heads),
        in_specs=[new_spec, new_spec, new_spec, cache_spec, cache_spec],
        out_specs=new_spec,
        out_shape=jax.ShapeDtypeStruct((batch * t_new, SB_W), BF16),
        scratch_shapes=[pltpu.VMEM((heads, t_new, 1), F32), pltpu.VMEM((heads, t_new, SB_HD), F32)],
        compiler_params=_params(("parallel", "parallel")),
        name="sb_sample",
    )(q, k_new, v_new, k_cache, v_cache)


SWA_SCALE = SWA_HD ** -0.5
SWA_PAIRS = SWA_GROUP // 2
SWA_PAIR_W = 2 * SWA_HD
SWA_COLS = SWA_PAIRS * CHUNK
SWA_BIAS_VARIANTS = 3
SWA_CHUNKS_PER_STEP = 8


def _swa_fill(dst_ref, front_ref, body_ref, seq):
    for g in range(SWA_KV_HEADS):
        for (lo, hi), src_ref in (((0, WINDOW), front_ref), ((WINDOW, WINDOW + seq), body_ref)):
            x = src_ref[:, g * SWA_HD:(g + 1) * SWA_HD]
            zero = jnp.zeros_like(x)
            dst_ref[g, 0, lo:hi, :] = jnp.concatenate([x, zero], axis=1).astype(BF16)
            dst_ref[g, 1, lo:hi, :] = jnp.concatenate([zero, x], axis=1).astype(BF16)


def _swa_softmax(s, sk):
    m = jnp.maximum(jnp.max(s, axis=0, keepdims=True), sk)
    p = jnp.exp(s - m)
    den = jnp.sum(p, axis=0, keepdims=True) + jnp.exp(sk - m)
    return p.astype(BF16), 1.0 / den


def _swa_kernel(q_ref, k_ref, v_ref, kf_ref, vf_ref, bias_ref, sink_ref, o_ref, kab_ref, vab_ref,
                *, chunks, seq, front_valid):
    t = pl.program_id(1)

    @pl.when(t == 0)
    def _():
        _swa_fill(kab_ref, kf_ref, k_ref, seq)
        _swa_fill(vab_ref, vf_ref, v_ref, seq)

    blocks = [(c, g) for c in range(chunks) for g in range(SWA_KV_HEADS)]

    def band(ref, c, g):
        rows = pl.ds(pl.multiple_of((t * chunks + c) * CHUNK, CHUNK), BAND)
        return jnp.concatenate([ref[g, 0, rows, :], ref[g, 1, rows, :]], axis=0)

    def q_pairs(c, g):
        return jnp.concatenate(
            [q_ref[c * CHUNK:(c + 1) * CHUNK, (g * SWA_PAIRS + r) * SWA_PAIR_W:(g * SWA_PAIRS + r + 1) * SWA_PAIR_W]
             for r in range(SWA_PAIRS)], axis=0)

    def variant(c):
        return SWA_BIAS_VARIANTS - 1 if front_valid else jnp.minimum(t * chunks + c, SWA_BIAS_VARIANTS - 1)

    s = [_dot_nt(band(kab_ref, c, g), q_pairs(c, g)) * SWA_SCALE + bias_ref[variant(c), g] for c, g in blocks]
    p = [(_swa_softmax(x[:BAND], sink_ref[g, 0]), _swa_softmax(x[BAND:], sink_ref[g, 1])) for x, (c, g) in zip(s, blocks)]
    pv = [lax.dot_general(band(vab_ref, c, g), jnp.concatenate([even[0], odd[0]], axis=0),
                          (((0,), (0,)), ((), ())), preferred_element_type=F32)
          for (even, odd), (c, g) in zip(p, blocks)]
    o = [jnp.concatenate([x[:SWA_HD] * even[1], x[SWA_HD:] * odd[1]], axis=0).T
         for x, (even, odd) in zip(pv, p)]
    for x, (c, g) in zip(o, blocks):
        for r in range(SWA_PAIRS):
            lanes = slice((g * SWA_PAIRS + r) * SWA_PAIR_W, (g * SWA_PAIRS + r + 1) * SWA_PAIR_W)
            o_ref[c * CHUNK:(c + 1) * CHUNK, lanes] = x[r * CHUNK:(r + 1) * CHUNK].astype(o_ref.dtype)


def _swa(q, k, v, k_front, v_front, bias, sink, batch, seq, front_valid):
    chunks = min(SWA_CHUNKS_PER_STEP, seq // CHUNK)
    tile = chunks * CHUNK
    nt = seq // tile
    kv_spec = pl.BlockSpec((seq, SWA_KV_W), lambda b, t: (b, 0))
    front_spec = pl.BlockSpec((WINDOW, SWA_KV_W), lambda b, t: (b, 0))
    return pl.pallas_call(
        functools.partial(_swa_kernel, chunks=chunks, seq=seq, front_valid=front_valid),
        grid=(batch, nt),
        in_specs=[pl.BlockSpec((tile, SWA_W), lambda b, t: (b * nt + t, 0)),
                  kv_spec, kv_spec, front_spec, front_spec,
                  _resident(bias.shape), _resident(sink.shape)],
        out_specs=pl.BlockSpec((tile, SWA_W), lambda b, t: (b * nt + t, 0)),
        out_shape=jax.ShapeDtypeStruct((batch * seq, SWA_W), BF16),
        scratch_shapes=[pltpu.VMEM((SWA_KV_HEADS, 2, seq + WINDOW, SWA_PAIR_W), BF16)] * 2,
        compiler_params=_params(("parallel", "arbitrary")),
        name="swa",
    )(q, k, v, k_front, v_front, bias, sink)


def _t5_bucket(rel):
    nb = NUM_BUCKETS // 2
    max_exact = nb // 2
    ret = np.where(rel > 0, nb, 0)
    n = np.abs(rel)
    nf = np.maximum(n, 1).astype(np.float32)
    large = max_exact + (np.log(nf / np.float32(max_exact)) / np.float32(math.log(MAX_DISTANCE / max_exact))
                         * np.float32(nb - max_exact)).astype(np.int32)
    large = np.minimum(large, nb - 1)
    return ret + np.where(n < max_exact, n, large)


def _band_bias(rel_bias):
    rel = np.arange(BAND)[None, :] - WINDOW - np.arange(CHUNK)[:, None]
    onehot = np.eye(NUM_BUCKETS, dtype=np.float32)[_t5_bucket(rel).reshape(-1)]
    b = jnp.dot(jnp.asarray(onehot), rel_bias.astype(F32), precision=lax.Precision.HIGHEST)
    b = b.reshape(CHUNK, BAND, SWA_KV_HEADS, SWA_PAIRS, 2)
    b = jnp.transpose(b, (2, 4, 1, 3, 0)).reshape(SWA_KV_HEADS, 2 * BAND, SWA_COLS)
    first_valid = np.array([WINDOW - v * CHUNK for v in range(SWA_BIAS_VARIANTS)])
    key = np.tile(np.arange(BAND), 2)
    mask = np.where(key[None, :] < first_valid[:, None], NEG_INF, 0.0).astype(np.float32)
    return b[None] + jnp.asarray(mask)[:, None, :, None]


def _sink_cols(sink):
    s = jnp.transpose(sink.astype(F32).reshape(SWA_KV_HEADS, SWA_PAIRS, 2), (0, 2, 1))
    return jnp.repeat(s, CHUNK, axis=-1).reshape(SWA_KV_HEADS, 2, 1, SWA_COLS)


MEM_SCALE = MEM_HD ** -0.5


def _mem_attn_kernel(q_ref, k_ref, v_ref, o_ref):
    cols = [slice(h * MEM_HD, (h + 1) * MEM_HD) for h in range(MEM_HEADS)]
    ones = jnp.ones((MEM_LEN, MEM_HD), BF16)
    s = [_dot_nt(q_ref[:, cs], k_ref[0, :, cs].astype(BF16)) * MEM_SCALE for cs in cols]
    p = [jnp.exp(x - jnp.max(x, axis=-1, keepdims=True)).astype(BF16) for x in s]
    pv = [_dot(x, jnp.concatenate([v_ref[0, :, cs].astype(BF16), ones], axis=1)) for x, cs in zip(p, cols)]
    for x, cs in zip(pv, cols):
        o_ref[:, cs] = (x[:, :MEM_HD] * (1.0 / x[:, MEM_HD:])).astype(o_ref.dtype)


def _mem_attn(qm, mk, mv, layer, batch, seq, tq):
    nt = seq // tq
    kv_spec = pl.BlockSpec((1, MEM_LEN, MEM_W), lambda b, t: (layer, b, 0))
    return pl.pallas_call(
        _mem_attn_kernel,
        grid=(batch, nt),
        in_specs=[pl.BlockSpec((tq, MEM_W), lambda b, t: (b * nt + t, 0)), kv_spec, kv_spec],
        out_specs=pl.BlockSpec((tq, MEM_W), lambda b, t: (b * nt + t, 0)),
        out_shape=jax.ShapeDtypeStruct((batch * seq, MEM_W), BF16),
        compiler_params=_params(("parallel", "parallel")),
        name="mem_attn",
    )(qm, mk, mv)


OUT_PROJ_SUBTILES = 2


def _out_proj_kernel(o_ref, om_ref, gate_ref, x_ref, w_ref, g_ref, y_ref, *, mix_w):
    rows = o_ref.shape[0] // OUT_PROJ_SUBTILES
    subs = [slice(r * rows, (r + 1) * rows) for r in range(OUT_PROJ_SUBTILES)]
    y = [_dot(o_ref[rs, :] * gate_ref[rs, :mix_w], w_ref[:mix_w, :])
         + _dot(om_ref[rs, :] * gate_ref[rs, mix_w:], w_ref[mix_w:, :]) for rs in subs]
    for rs, yy in zip(subs, y):
        y_ref[rs, :] = x_ref[rs, :] + _rmsnorm(yy, g_ref[...])


def _out_proj(o, om, gate, x, w, gain, tm):
    n_tok, d = x.shape
    mix_w = o.shape[1]
    row = lambda width: pl.BlockSpec((tm, width), lambda i: (i, 0))
    return pl.pallas_call(
        functools.partial(_out_proj_kernel, mix_w=mix_w),
        grid=(n_tok // tm,),
        in_specs=[row(mix_w), row(MEM_W), row(mix_w + MEM_W), row(d),
                  _resident(w.shape), _resident((1, d))],
        out_specs=row(d),
        out_shape=jax.ShapeDtypeStruct((n_tok, d), F32),
        compiler_params=_params(("parallel",)),
        name="out_proj",
    )(o, om, gate, x, w, gain.reshape(1, d))


def _split_cols(w, widths):
    out, off = [], 0
    for width in widths:
        out.append(w[:, off:off + width].astype(BF16))
        off += width
    return out


@jax.jit
def kernel(x_prompt, x_sample, mem_prompt, cache_sb_k, cache_sb_v, cache_swa_k, cache_swa_v,
           cache_mem_k, cache_mem_v, pre_norm, post_norm, mem_norm, w_in_a, w_in_b, w_mem_kv,
           w_out, rel_bias, sinks):
    batch, seq, d = x_prompt.shape
    dec_batch, t_new, _ = x_sample.shape
    depth = pre_norm.shape[0]
    past = cache_sb_k.shape[2]
    n_p, n_s = batch * seq, dec_batch * t_new
    d_inner = w_out.shape[1]

    xp = x_prompt.reshape(n_p, d)
    xs = x_sample.reshape(n_s, d)
    proj_dtypes = (BF16, F32, F32, BF16, BF16)
    gate = (4,)
    widths_a = (SB_W, SB_W, SB_W, MEM_W, d_inner)
    widths_b = (SWA_W, SWA_KV_W, SWA_KV_W, MEM_W, d_inner)

    mem_k, mem_v = _mem_proj(mem_prompt.reshape(batch * MEM_LEN, d), mem_norm,
                             w_mem_kv.astype(BF16), tm=512)
    bias = _band_bias(rel_bias)
    zero_front = jnp.zeros((batch * WINDOW, SWA_KV_W), F32)

    n_a, n_b = (depth + 1) // 2, depth // 2
    sb_state_p = None
    heads_major = (0, 1, 3, 2, 4)
    cache_k = jnp.transpose(cache_sb_k, heads_major)
    cache_v = jnp.transpose(cache_sb_v, heads_major)
    cache_mk = cache_mem_k.reshape(depth, dec_batch * MEM_LEN, MEM_W)
    cache_mv = cache_mem_v.reshape(depth, dec_batch * MEM_LEN, MEM_W)
    sb_ks, sb_vs = [], []
    swa_kp, swa_vp, swa_ks, swa_vs = [], [], [], []
    for i in range(depth):
        j = i // 2
        if i % 2 == 0:
            weights = _split_cols(w_in_a[j], widths_a)
            (q_p, k_p, v_p, qm_p, z_p), sb_state_p = _norm_proj(
                xp, pre_norm[i], weights, (BF16, BF16, BF16, BF16, BF16),
                tm=256 if sb_state_p is None else 512,
                gate_idx=gate,
                state_idx=(1, 2), slab=j, n_slabs=n_a, seq=seq, prev_states=sb_state_p)
            (q_s, k_s, v_s, qm_s, z_s), _ = _norm_proj(xs, pre_norm[i], weights, proj_dtypes, tm=256,
                                                       gate_idx=gate)
            o_p = _sb_prompt(q_p, k_p, v_p, batch, seq)
            o_s = _sb_sample(q_s, k_s, v_s, cache_k, cache_v, j, t_new)
            sb_ks.append(k_s)
            sb_vs.append(v_s)
        else:
            weights = _split_cols(w_in_b[j], widths_b)
            (q_p, k_p, v_p, qm_p, z_p), _ = _norm_proj(xp, pre_norm[i], weights, proj_dtypes, tm=512,
                                                       gate_idx=gate)
            (q_s, k_s, v_s, qm_s, z_s), _ = _norm_proj(xs, pre_norm[i], weights, proj_dtypes, tm=256,
                                                       gate_idx=gate)
            sink = _sink_cols(sinks[j])
            o_p = _swa(q_p, k_p, v_p, zero_front, zero_front, bias, sink, batch, seq, front_valid=False)
            kc = cache_swa_k[j].reshape(dec_batch * WINDOW, SWA_KV_W)
            vc = cache_swa_v[j].reshape(dec_batch * WINDOW, SWA_KV_W)
            o_s = _swa(q_s, k_s, v_s, kc, vc, bias, sink, dec_batch, t_new, front_valid=True)
            swa_kp.append(k_p.reshape(batch, seq, SWA_KV_W)[:, seq - WINDOW:])
            swa_vp.append(v_p.reshape(batch, seq, SWA_KV_W)[:, seq - WINDOW:])
            k_all = jnp.concatenate([kc.reshape(dec_batch, WINDOW, SWA_KV_W),
                                     k_s.reshape(dec_batch, t_new, SWA_KV_W)], axis=1)
            v_all = jnp.concatenate([vc.reshape(dec_batch, WINDOW, SWA_KV_W),
                                     v_s.reshape(dec_batch, t_new, SWA_KV_W)], axis=1)
            swa_ks.append(k_all[:, t_new:])
            swa_vs.append(v_all[:, t_new:])
        om_p = _mem_attn(qm_p, mem_k, mem_v, i, batch, seq, tq=512)
        om_s = _mem_attn(qm_s, cache_mk, cache_mv, i, dec_batch, t_new, tq=t_new)
        w_o = w_out[i].astype(BF16)
        xp = _out_proj(o_p, om_p, z_p, xp, w_o, post_norm[i], tm=1024)
        xs = _out_proj(o_s, om_s, z_s, xs, w_o, post_norm[i], tm=256)

    return (xp.reshape(batch, seq, d), xs.reshape(dec_batch, t_new, d),
            jnp.transpose(sb_state_p[0], heads_major),
            jnp.transpose(sb_state_p[1], heads_major),
            jnp.stack(sb_ks).reshape(n_a, dec_batch, t_new, SB_HEADS, SB_HD),
            jnp.stack(sb_vs).reshape(n_a, dec_batch, t_new, SB_HEADS, SB_HD),
            jnp.stack(swa_kp).reshape(n_b, batch, WINDOW, SWA_KV_HEADS, SWA_HD),
            jnp.stack(swa_vp).reshape(n_b, batch, WINDOW, SWA_KV_HEADS, SWA_HD),
            jnp.stack(swa_ks).reshape(n_b, dec_batch, WINDOW, SWA_KV_HEADS, SWA_HD),
            jnp.stack(swa_vs).reshape(n_b, dec_batch, WINDOW, SWA_KV_HEADS, SWA_HD),
            mem_k.reshape(depth, batch, MEM_LEN, MEM_HEADS, MEM_HD),
            mem_v.reshape(depth, batch, MEM_LEN, MEM_HEADS, MEM_HD))
```

```python
import functools
import math

import numpy as np
import jax
import jax.numpy as jnp
from jax import lax
from jax.experimental import pallas as pl
from jax.experimental.pallas import tpu as pltpu

F32 = jnp.float32
BF16 = jnp.bfloat16

D_MODEL = 1024
CHUNK = 64
MEM_LEN = 256
MEM_HEADS = 4
MEM_HD = 128
MEM_W = MEM_HEADS * MEM_HD
SB_HEADS = 12
SB_HD = 128
SB_W = SB_HEADS * SB_HD
SWA_HEADS = 24
SWA_KV_HEADS = 3
SWA_GROUP = SWA_HEADS // SWA_KV_HEADS
SWA_HD = 64
SWA_W = SWA_HEADS * SWA_HD
SWA_KV_W = SWA_KV_HEADS * SWA_HD
WINDOW = 128
BAND = WINDOW + CHUNK
NUM_BUCKETS = 32
MAX_DISTANCE = 128
EPS = 1e-6
NEG_INF = -1e30

VMEM_LIMIT_BYTES = 56 * 1024 * 1024


def _params(semantics):
    return pltpu.CompilerParams(dimension_semantics=semantics,
                                vmem_limit_bytes=VMEM_LIMIT_BYTES)


def _resident(shape):
    nd = len(shape)
    return pl.BlockSpec(shape, lambda *_: (0,) * nd, pipeline_mode=pl.Buffered(1))


def _rmsnorm(x, g):
    return x * lax.rsqrt(jnp.mean(x * x, axis=-1, keepdims=True) + EPS) * g


def _dot(a, b):
    return jnp.dot(a, b, preferred_element_type=F32)


def _dot_nt(a, b):
    return lax.dot_general(a, b, (((1,), (1,)), ((), ())), preferred_element_type=F32)


PROJ_COL_CHUNK = 512
PROJ_SUBTILES = 2


def _silu(z):
    return z * (1.0 / (1.0 + jnp.exp(-z)))


def _norm_proj_kernel(x_ref, g_ref, *refs, widths, state_idx, gate_idx, n_prev, slab):
    n = len(widths)
    w_refs = refs[:n]
    o_refs = refs[n + n_prev:2 * n + n_prev]
    s_refs = refs[2 * n + n_prev:]
    s_slab = 0 if n_prev else slab
    rows = x_ref.shape[0] // PROJ_SUBTILES
    subs = [slice(r * rows, (r + 1) * rows) for r in range(PROJ_SUBTILES)]
    h = [_rmsnorm(x_ref[rs, :], g_ref[...]).astype(BF16) for rs in subs]
    for idx, (w_ref, o_ref, width) in enumerate(zip(w_refs, o_refs, widths)):
        s_ref = s_refs[state_idx.index(idx)] if idx in state_idx else None
        for c in range(0, width, PROJ_COL_CHUNK):
            cw = min(PROJ_COL_CHUNK, width - c)
            for rs, hs in zip(subs, h):
                y = _dot(hs, w_ref[:, c:c + cw])
                o_ref[rs, c:c + cw] = (_silu(y) if idx in gate_idx else y).astype(o_ref.dtype)
                if s_ref is not None:
                    for hh in range(cw // SB_HD):
                        s_ref[s_slab, 0, c // SB_HD + hh, rs, :] = y[:, hh * SB_HD:(hh + 1) * SB_HD]
    for s_ref in s_refs:
        for other in range(s_ref.shape[0]):
            if other != s_slab:
                s_ref[other] = jnp.zeros(s_ref.shape[1:], s_ref.dtype)


def _norm_proj(x, gain, weights, dtypes, tm, gate_idx=(), state_idx=(), slab=0, n_slabs=1, seq=None,
               prev_states=None):
    n_tok, d = x.shape
    widths = tuple(w.shape[1] for w in weights)
    prev = tuple(prev_states) if prev_states is not None else ()
    n_in = 2 + len(weights)
    state_specs, state_shapes = [], []
    if state_idx:
        nt = seq // tm
        if prev:
            spec = pl.BlockSpec((1, 1, SB_HEADS, tm, SB_HD), lambda i: (slab, i // nt, 0, i % nt, 0))
        else:
            spec = pl.BlockSpec((n_slabs, 1, SB_HEADS, tm, SB_HD), lambda i: (0, i // nt, 0, i % nt, 0))
        state_specs = [spec for _ in state_idx]
        state_shapes = [jax.ShapeDtypeStruct((n_slabs, n_tok // seq, SB_HEADS, seq, SB_HD), F32)
                        for _ in state_idx]
    outs = pl.pallas_call(
        functools.partial(_norm_proj_kernel, widths=widths, state_idx=tuple(state_idx),
                          gate_idx=tuple(gate_idx), n_prev=len(prev), slab=slab),
        grid=(n_tok // tm,),
        in_specs=[pl.BlockSpec((tm, d), lambda i: (i, 0)), _resident((1, d))]
        + [_resident(w.shape) for w in weights]
        + [pl.BlockSpec(memory_space=pl.ANY) for _ in prev],
        out_specs=[pl.BlockSpec((tm, w), lambda i: (i, 0)) for w in widths] + state_specs,
        out_shape=[jax.ShapeDtypeStruct((n_tok, w), dt) for w, dt in zip(widths, dtypes)] + state_shapes,
        input_output_aliases={n_in + s: len(widths) + s for s in range(len(prev))},
        compiler_params=_params(("parallel",)),
        name="norm_proj",
    )(x, gain.reshape(1, d), *weights, *prev)
    return outs[:len(widths)], outs[len(widths):]


def _mem_proj_kernel(x_ref, g_ref, w_ref, k_ref, v_ref):
    h = _rmsnorm(x_ref[...], g_ref[0]).astype(BF16)
    k_ref[0] = _dot(h, w_ref[0, :, :MEM_W])
    v_ref[0] = _dot(h, w_ref[0, :, MEM_W:])


def _mem_proj(mem, gains, w, tm):
    n_tok, d = mem.shape
    depth = w.shape[0]
    out = jax.ShapeDtypeStruct((depth, n_tok, MEM_W), F32)
    return pl.pallas_call(
        _mem_proj_kernel,
        grid=(depth, n_tok // tm),
        in_specs=[pl.BlockSpec((tm, d), lambda l, i: (i, 0)),
                  pl.BlockSpec((1, 1, d), lambda l, i: (l, 0, 0)),
                  pl.BlockSpec((1, d, 2 * MEM_W), lambda l, i: (l, 0, 0))],
        out_specs=[pl.BlockSpec((1, tm, MEM_W), lambda l, i: (l, i, 0))] * 2,
        out_shape=[out, out],
        compiler_params=_params(("parallel", "parallel")),
        name="mem_proj",
    )(mem, gains.reshape(depth, 1, d), w)


LOG2E = math.log2(math.e)
SB_LOG2_SCALE = SB_HD ** -0.5 * LOG2E
SB_TQ = 256
SB_TK = 256
SB_HEADS_PER_STEP = 6
SB_SAMPLE_HEADS_PER_STEP = 3
SB_LIVE_ROWS = 160
SB_DEAD_LOG2 = 151.0


def _later_key_matrix(n):
    later = lax.broadcasted_iota(jnp.int32, (n, n), 0) > lax.broadcasted_iota(jnp.int32, (n, n), 1)
    return jnp.where(later, 1.0, 0.0).astype(BF16)


def _strict_causal(n_rows, n_cols, first_row):
    return (lax.broadcasted_iota(jnp.int32, (n_rows, n_cols), 1)
            < lax.broadcasted_iota(jnp.int32, (n_rows, n_cols), 0) + first_row)


def _sb_visit(chains):
    z2 = [_dot_nt(c[0], c[1]) * SB_LOG2_SCALE for c in chains]
    nl = [jnp.maximum(z, 0.0) + jnp.log2(1.0 + jnp.exp2(-jnp.abs(z))) for z in z2]
    nl = [x if c[4] is None else jnp.where(c[4], x, 0.0) for x, c in zip(nl, chains)]
    ncarry = [c[5] + jnp.sum(x, axis=-1, keepdims=True) for x, c in zip(nl, chains)]
    head = [z - x - c[5] for z, x, c in zip(z2, nl, chains)]
    a = [jnp.exp2(t - _dot(x.astype(BF16), c[3])) for t, x, c in zip(head, nl, chains)]
    a = [x if c[4] is None else jnp.where(c[4], x, 0.0) for x, c in zip(a, chains)]
    acc = [c[6] + _dot(x.astype(BF16), c[2]) for x, c in zip(a, chains)]
    return tuple(zip(ncarry, acc))


def _sb_older_blocks(n_blocks, visit, nc_ref, acc_ref, live_rows=None):
    heads, total = nc_ref.shape[0], nc_ref.shape[1]

    def liveness():
        least = functools.reduce(jnp.minimum, [nc_ref[c] for c in range(heads)])
        if live_rows is None:
            return (jnp.min(least) < SB_DEAD_LOG2).astype(jnp.int32), jnp.int32(0)
        lead, tail = jnp.min(least[:live_rows]), jnp.min(least[live_rows:])
        return ((jnp.minimum(lead, tail) < SB_DEAD_LOG2).astype(jnp.int32),
                (tail >= SB_DEAD_LOG2).astype(jnp.int32))

    def step(block, rows):
        new = visit(block, rows, tuple((nc_ref[c, :rows], acc_ref[c, :rows]) for c in range(heads)))
        _sb_store(nc_ref, acc_ref, new, rows)

    def cond(loop):
        return jnp.logical_and(loop[0] < n_blocks, loop[1] > 0)

    def body(loop):
        block, _, tail_dead = loop
        if live_rows is None:
            step(block, total)
        else:
            pl.when(tail_dead > 0)(lambda: step(block, live_rows))
            pl.when(tail_dead == 0)(lambda: step(block, total))
        return (block + 1,) + liveness()

    lax.while_loop(cond, body, (jnp.int32(0),) + liveness())


def _sb_store(nc_ref, acc_ref, state, rows):
    for c, (ncarry, acc) in enumerate(state):
        nc_ref[c, :rows] = ncarry
        acc_ref[c, :rows] = acc


def _sb_zero(rows):
    return jnp.zeros((rows, 1), F32), jnp.zeros((rows, SB_HD), F32)


def _sb_prompt_kernel(q_ref, k_ref, v_ref, gate_ref, o_ref, nc_ref, acc_ref, *, heads):
    i = pl.program_id(2)
    cols = [slice(c * SB_HD, (c + 1) * SB_HD) for c in range(heads)]
    later = _later_key_matrix(SB_TK)
    diag = pl.ds(pl.multiple_of(i * SB_TK, SB_TK), SB_TK)
    mask = _strict_causal(SB_TQ, SB_TK, 0)
    _sb_store(nc_ref, acc_ref,
              _sb_visit([(q_ref[:, cs], k_ref[diag, cs], v_ref[diag, cs], later, mask) + _sb_zero(SB_TQ)
                         for cs in cols]), SB_TQ)

    def older(jj, rows, st):
        keys = pl.ds(pl.multiple_of((i - 1 - jj) * SB_TK, SB_TK), SB_TK)
        return _sb_visit([(q_ref[:rows, cs], k_ref[keys, cs], v_ref[keys, cs], later, None) + tuple(s)
                          for cs, s in zip(cols, st)])

    _sb_older_blocks(i, older, nc_ref, acc_ref, live_rows=SB_LIVE_ROWS)
    for c, cs in enumerate(cols):
        o_ref[:, cs] = (acc_ref[c] * gate_ref[:, cs]).astype(o_ref.dtype)


def _sb_prompt(q, k, v, gate, batch, seq):
    n_tok = batch * seq
    nq = seq // SB_TQ
    heads = SB_HEADS_PER_STEP
    width = heads * SB_HD
    return pl.pallas_call(
        functools.partial(_sb_prompt_kernel, heads=heads),
        grid=(batch, SB_HEADS // heads, nq),
        in_specs=[pl.BlockSpec((SB_TQ, width), lambda b, h, i: (b * nq + i, h)),
                  pl.BlockSpec((seq, width), lambda b, h, i: (b, h)),
                  pl.BlockSpec((seq, width), lambda b, h, i: (b, h)),
                  pl.BlockSpec((SB_TQ, width), lambda b, h, i: (b * nq + i, h))],
        out_specs=pl.BlockSpec((SB_TQ, width), lambda b, h, i: (b * nq + i, h)),
        out_shape=jax.ShapeDtypeStruct((n_tok, SB_W), BF16),
        scratch_shapes=[pltpu.VMEM((heads, SB_TQ, 1), F32), pltpu.VMEM((heads, SB_TQ, SB_HD), F32)],
        compiler_params=_params(("parallel", "parallel", "parallel")),
        name="sb_prompt",
    )(q, k, v, gate)


def _sb_sample_kernel(q_ref, kn_ref, vn_ref, kc_ref, vc_ref, gate_ref, o_ref, nc_ref, acc_ref,
                      *, heads, t_new, past):
    cols = [slice(c * SB_HD, (c + 1) * SB_HD) for c in range(heads)]
    later_new, mask_new = _later_key_matrix(t_new), _strict_causal(t_new, t_new, 0)
    _sb_store(nc_ref, acc_ref,
              _sb_visit([(q_ref[:, cs], kn_ref[:, cs].astype(BF16), vn_ref[:, cs].astype(BF16), later_new, mask_new)
                         + _sb_zero(t_new) for cs in cols]), t_new)
    later = _later_key_matrix(SB_TK)
    n_blocks = past // SB_TK

    def older(jj, rows, st):
        keys = pl.ds(pl.multiple_of((n_blocks - 1 - jj) * SB_TK, SB_TK), SB_TK)
        return _sb_visit([(q_ref[:, cs], kc_ref[0, 0, c, keys, :].astype(BF16), vc_ref[0, 0, c, keys, :].astype(BF16),
                           later, None) + tuple(s) for c, (cs, s) in enumerate(zip(cols, st))])

    _sb_older_blocks(n_blocks, older, nc_ref, acc_ref)
    for c, cs in enumerate(cols):
        o_ref[:, cs] = (acc_ref[c] * gate_ref[:, cs]).astype(o_ref.dtype)


def _sb_sample(q, k_new, v_new, k_cache, v_cache, gate, layer, t_new):
    _, batch, _, past, _ = k_cache.shape
    heads = SB_SAMPLE_HEADS_PER_STEP
    width = heads * SB_HD
    new_spec = pl.BlockSpec((t_new, width), lambda b, h: (b, h))
    cache_spec = pl.BlockSpec((1, 1, heads, past, SB_HD), lambda b, h: (layer, b, h, 0, 0))
    return pl.pallas_call(
        functools.partial(_sb_sample_kernel, heads=heads, t_new=t_new, past=past),
        grid=(batch, SB_HEADS // heads),
        in_specs=[new_spec, new_spec, new_spec, cache_spec, cache_spec, new_spec],
        out_specs=new_spec,
        out_shape=jax.ShapeDtypeStruct((batch * t_new, SB_W), BF16),
        scratch_shapes=[pltpu.VMEM((heads, t_new, 1), F32), pltpu.VMEM((heads, t_new, SB_HD), F32)],
        compiler_params=_params(("parallel", "parallel")),
        name="sb_sample",
    )(q, k_new, v_new, k_cache, v_cache, gate)


SWA_SCALE = SWA_HD ** -0.5
SWA_PAIRS = SWA_GROUP // 2
SWA_PAIR_W = 2 * SWA_HD
SWA_COLS = SWA_PAIRS * CHUNK
SWA_BIAS_VARIANTS = 3
SWA_CHUNKS_PER_STEP = 8


def _swa_fill(dst_ref, front_ref, body_ref, seq):
    for g in range(SWA_KV_HEADS):
        for (lo, hi), src_ref in (((0, WINDOW), front_ref), ((WINDOW, WINDOW + seq), body_ref)):
            x = src_ref[:, g * SWA_HD:(g + 1) * SWA_HD]
            zero = jnp.zeros_like(x)
            dst_ref[g, 0, lo:hi, :] = jnp.concatenate([x, zero], axis=1).astype(BF16)
            dst_ref[g, 1, lo:hi, :] = jnp.concatenate([zero, x], axis=1).astype(BF16)


def _swa_softmax(s, sk):
    m = jnp.maximum(jnp.max(s, axis=0, keepdims=True), sk)
    p = jnp.exp(s - m)
    den = jnp.sum(p, axis=0, keepdims=True) + jnp.exp(sk - m)
    return p.astype(BF16), 1.0 / den


def _swa_kernel(q_ref, k_ref, v_ref, kf_ref, vf_ref, bias_ref, sink_ref, gate_ref, o_ref, kab_ref, vab_ref,
                *, chunks, seq, front_valid):
    t = pl.program_id(1)

    @pl.when(t == 0)
    def _():
        _swa_fill(kab_ref, kf_ref, k_ref, seq)
        _swa_fill(vab_ref, vf_ref, v_ref, seq)

    blocks = [(c, g) for c in range(chunks) for g in range(SWA_KV_HEADS)]

    def band(ref, c, g):
        rows = pl.ds(pl.multiple_of((t * chunks + c) * CHUNK, CHUNK), BAND)
        return jnp.concatenate([ref[g, 0, rows, :], ref[g, 1, rows, :]], axis=0)

    def q_pairs(c, g):
        return jnp.concatenate(
            [q_ref[c * CHUNK:(c + 1) * CHUNK, (g * SWA_PAIRS + r) * SWA_PAIR_W:(g * SWA_PAIRS + r + 1) * SWA_PAIR_W]
             for r in range(SWA_PAIRS)], axis=0)

    def variant(c):
        return SWA_BIAS_VARIANTS - 1 if front_valid else jnp.minimum(t * chunks + c, SWA_BIAS_VARIANTS - 1)

    s = [_dot_nt(band(kab_ref, c, g), q_pairs(c, g)) * SWA_SCALE + bias_ref[variant(c), g] for c, g in blocks]
    p = [(_swa_softmax(x[:BAND], sink_ref[g, 0]), _swa_softmax(x[BAND:], sink_ref[g, 1])) for x, (c, g) in zip(s, blocks)]
    pv = [lax.dot_general(band(vab_ref, c, g), jnp.concatenate([even[0], odd[0]], axis=0),
                          (((0,), (0,)), ((), ())), preferred_element_type=F32)
          for (even, odd), (c, g) in zip(p, blocks)]
    o = [jnp.concatenate([x[:SWA_HD] * even[1], x[SWA_HD:] * odd[1]], axis=0).T
         for x, (even, odd) in zip(pv, p)]
    for x, (c, g) in zip(o, blocks):
        for r in range(SWA_PAIRS):
            lanes = slice((g * SWA_PAIRS + r) * SWA_PAIR_W, (g * SWA_PAIRS + r + 1) * SWA_PAIR_W)
            rows = slice(c * CHUNK, (c + 1) * CHUNK)
            o_ref[rows, lanes] = (x[r * CHUNK:(r + 1) * CHUNK] * gate_ref[rows, lanes]).astype(o_ref.dtype)


def _swa(q, k, v, k_front, v_front, bias, sink, gate, batch, seq, front_valid):
    chunks = min(SWA_CHUNKS_PER_STEP, seq // CHUNK)
    tile = chunks * CHUNK
    nt = seq // tile
    kv_spec = pl.BlockSpec((seq, SWA_KV_W), lambda b, t: (b, 0))
    front_spec = pl.BlockSpec((WINDOW, SWA_KV_W), lambda b, t: (b, 0))
    return pl.pallas_call(
        functools.partial(_swa_kernel, chunks=chunks, seq=seq, front_valid=front_valid),
        grid=(batch, nt),
        in_specs=[pl.BlockSpec((tile, SWA_W), lambda b, t: (b * nt + t, 0)),
                  kv_spec, kv_spec, front_spec, front_spec,
                  _resident(bias.shape), _resident(sink.shape),
                  pl.BlockSpec((tile, SWA_W), lambda b, t: (b * nt + t, 0))],
        out_specs=pl.BlockSpec((tile, SWA_W), lambda b, t: (b * nt + t, 0)),
        out_shape=jax.ShapeDtypeStruct((batch * seq, SWA_W), BF16),
        scratch_shapes=[pltpu.VMEM((SWA_KV_HEADS, 2, seq + WINDOW, SWA_PAIR_W), BF16)] * 2,
        compiler_params=_params(("parallel", "arbitrary")),
        name="swa",
    )(q, k, v, k_front, v_front, bias, sink, gate)


def _t5_bucket(rel):
    nb = NUM_BUCKETS // 2
    max_exact = nb // 2
    ret = np.where(rel > 0, nb, 0)
    n = np.abs(rel)
    nf = np.maximum(n, 1).astype(np.float32)
    large = max_exact + (np.log(nf / np.float32(max_exact)) / np.float32(math.log(MAX_DISTANCE / max_exact))
                         * np.float32(nb - max_exact)).astype(np.int32)
    large = np.minimum(large, nb - 1)
    return ret + np.where(n < max_exact, n, large)


def _band_bias(rel_bias):
    rel = np.arange(BAND)[None, :] - WINDOW - np.arange(CHUNK)[:, None]
    onehot = np.eye(NUM_BUCKETS, dtype=np.float32)[_t5_bucket(rel).reshape(-1)]
    b = jnp.dot(jnp.asarray(onehot), rel_bias.astype(F32), precision=lax.Precision.HIGHEST)
    b = b.reshape(CHUNK, BAND, SWA_KV_HEADS, SWA_PAIRS, 2)
    b = jnp.transpose(b, (2, 4, 1, 3, 0)).reshape(SWA_KV_HEADS, 2 * BAND, SWA_COLS)
    first_valid = np.array([WINDOW - v * CHUNK for v in range(SWA_BIAS_VARIANTS)])
    key = np.tile(np.arange(BAND), 2)
    mask = np.where(key[None, :] < first_valid[:, None], NEG_INF, 0.0).astype(np.float32)
    return b[None] + jnp.asarray(mask)[:, None, :, None]


def _sink_cols(sink):
    s = jnp.transpose(sink.astype(F32).reshape(SWA_KV_HEADS, SWA_PAIRS, 2), (0, 2, 1))
    return jnp.repeat(s, CHUNK, axis=-1).reshape(SWA_KV_HEADS, 2, 1, SWA_COLS)


MEM_SCALE = MEM_HD ** -0.5


def _mem_attn_kernel(q_ref, k_ref, v_ref, gate_ref, o_ref):
    cols = [slice(h * MEM_HD, (h + 1) * MEM_HD) for h in range(MEM_HEADS)]
    ones = jnp.ones((MEM_LEN, MEM_HD), BF16)
    s = [_dot_nt(q_ref[:, cs], k_ref[0, :, cs].astype(BF16)) * MEM_SCALE for cs in cols]
    p = [jnp.exp(x - jnp.max(x, axis=-1, keepdims=True)).astype(BF16) for x in s]
    pv = [_dot(x, jnp.concatenate([v_ref[0, :, cs].astype(BF16), ones], axis=1)) for x, cs in zip(p, cols)]
    for x, cs in zip(pv, cols):
        o_ref[:, cs] = (x[:, :MEM_HD] * (1.0 / x[:, MEM_HD:]) * gate_ref[:, cs]).astype(o_ref.dtype)


def _mem_attn(qm, mk, mv, gate, layer, batch, seq, tq):
    nt = seq // tq
    gate_block = gate.shape[1] // MEM_W - 1
    kv_spec = pl.BlockSpec((1, MEM_LEN, MEM_W), lambda b, t: (layer, b, 0))
    return pl.pallas_call(
        _mem_attn_kernel,
        grid=(batch, nt),
        in_specs=[pl.BlockSpec((tq, MEM_W), lambda b, t: (b * nt + t, 0)), kv_spec, kv_spec,
                  pl.BlockSpec((tq, MEM_W), lambda b, t: (b * nt + t, gate_block))],
        out_specs=pl.BlockSpec((tq, MEM_W), lambda b, t: (b * nt + t, 0)),
        out_shape=jax.ShapeDtypeStruct((batch * seq, MEM_W), BF16),
        compiler_params=_params(("parallel", "parallel")),
        name="mem_attn",
    )(qm, mk, mv, gate)


OUT_PROJ_SUBTILES = 2


def _out_proj_kernel(u_ref, um_ref, x_ref, w_ref, g_ref, y_ref, *, mix_w):
    rows = u_ref.shape[0] // OUT_PROJ_SUBTILES
    subs = [slice(r * rows, (r + 1) * rows) for r in range(OUT_PROJ_SUBTILES)]
    y = [_dot(u_ref[rs, :], w_ref[:mix_w, :]) + _dot(um_ref[rs, :], w_ref[mix_w:, :]) for rs in subs]
    for rs, yy in zip(subs, y):
        y_ref[rs, :] = x_ref[rs, :] + _rmsnorm(yy, g_ref[...])


def _out_proj(u, um, x, w, gain, tm):
    n_tok, d = x.shape
    mix_w = u.shape[1]
    row = lambda width: pl.BlockSpec((tm, width), lambda i: (i, 0))
    return pl.pallas_call(
        functools.partial(_out_proj_kernel, mix_w=mix_w),
        grid=(n_tok // tm,),
        in_specs=[row(mix_w), row(MEM_W), row(d), _resident(w.shape), _resident((1, d))],
        out_specs=row(d),
        out_shape=jax.ShapeDtypeStruct((n_tok, d), F32),
        compiler_params=_params(("parallel",)),
        name="out_proj",
    )(u, um, x, w, gain.reshape(1, d))


def _split_cols(w, widths):
    out, off = [], 0
    for width in widths:
        out.append(w[:, off:off + width].astype(BF16))
        off += width
    return out


@jax.jit
def kernel(x_prompt, x_sample, mem_prompt, cache_sb_k, cache_sb_v, cache_swa_k, cache_swa_v,
           cache_mem_k, cache_mem_v, pre_norm, post_norm, mem_norm, w_in_a, w_in_b, w_mem_kv,
           w_out, rel_bias, sinks):
    batch, seq, d = x_prompt.shape
    dec_batch, t_new, _ = x_sample.shape
    depth = pre_norm.shape[0]
    past = cache_sb_k.shape[2]
    n_p, n_s = batch * seq, dec_batch * t_new
    d_inner = w_out.shape[1]

    xp = x_prompt.reshape(n_p, d)
    xs = x_sample.reshape(n_s, d)
    proj_dtypes = (BF16, F32, F32, BF16, BF16)
    gate = (4,)
    widths_a = (SB_W, SB_W, SB_W, MEM_W, d_inner)
    widths_b = (SWA_W, SWA_KV_W, SWA_KV_W, MEM_W, d_inner)

    mem_k, mem_v = _mem_proj(mem_prompt.reshape(batch * MEM_LEN, d), mem_norm,
                             w_mem_kv.astype(BF16), tm=512)
    bias = _band_bias(rel_bias)
    zero_front = jnp.zeros((batch * WINDOW, SWA_KV_W), F32)

    n_a, n_b = (depth + 1) // 2, depth // 2
    sb_state_p = None
    heads_major = (0, 1, 3, 2, 4)
    cache_k = jnp.transpose(cache_sb_k, heads_major)
    cache_v = jnp.transpose(cache_sb_v, heads_major)
    cache_mk = cache_mem_k.reshape(depth, dec_batch * MEM_LEN, MEM_W)
    cache_mv = cache_mem_v.reshape(depth, dec_batch * MEM_LEN, MEM_W)
    sb_ks, sb_vs = [], []
    swa_kp, swa_vp, swa_ks, swa_vs = [], [], [], []
    for i in range(depth):
        j = i // 2
        if i % 2 == 0:
            weights = _split_cols(w_in_a[j], widths_a)
            (q_p, k_p, v_p, qm_p, z_p), sb_state_p = _norm_proj(
                xp, pre_norm[i], weights, (BF16, BF16, BF16, BF16, BF16),
                tm=256 if sb_state_p is None else 512,
                gate_idx=gate,
                state_idx=(1, 2), slab=j, n_slabs=n_a, seq=seq, prev_states=sb_state_p)
            (q_s, k_s, v_s, qm_s, z_s), _ = _norm_proj(xs, pre_norm[i], weights, proj_dtypes, tm=256,
                                                       gate_idx=gate)
            o_p = _sb_prompt(q_p, k_p, v_p, z_p, batch, seq)
            o_s = _sb_sample(q_s, k_s, v_s, cache_k, cache_v, z_s, j, t_new)
            sb_ks.append(k_s)
            sb_vs.append(v_s)
        else:
            weights = _split_cols(w_in_b[j], widths_b)
            (q_p, k_p, v_p, qm_p, z_p), _ = _norm_proj(xp, pre_norm[i], weights, proj_dtypes, tm=512,
                                                       gate_idx=gate)
            (q_s, k_s, v_s, qm_s, z_s), _ = _norm_proj(xs, pre_norm[i], weights, proj_dtypes, tm=256,
                                                       gate_idx=gate)
            sink = _sink_cols(sinks[j])
            o_p = _swa(q_p, k_p, v_p, zero_front, zero_front, bias, sink, z_p, batch, seq, front_valid=False)
            kc = cache_swa_k[j].reshape(dec_batch * WINDOW, SWA_KV_W)
            vc = cache_swa_v[j].reshape(dec_batch * WINDOW, SWA_KV_W)
            o_s = _swa(q_s, k_s, v_s, kc, vc, bias, sink, z_s, dec_batch, t_new, front_valid=True)
            swa_kp.append(k_p.reshape(batch, seq, SWA_KV_W)[:, seq - WINDOW:])
            swa_vp.append(v_p.reshape(batch, seq, SWA_KV_W)[:, seq - WINDOW:])
            k_all = jnp.concatenate([kc.reshape(dec_batch, WINDOW, SWA_KV_W),
                                     k_s.reshape(dec_batch, t_new, SWA_KV_W)], axis=1)
            v_all = jnp.concatenate([vc.reshape(dec_batch, WINDOW, SWA_KV_W),
                                     v_s.reshape(dec_batch, t_new, SWA_KV_W)], axis=1)
            swa_ks.append(k_all[:, t_new:])
            swa_vs.append(v_all[:, t_new:])
        om_p = _mem_attn(qm_p, mem_k, mem_v, z_p, i, batch, seq, tq=1024)
        om_s = _mem_attn(qm_s, cache_mk, cache_mv, z_s, i, dec_batch, t_new, tq=t_new)
        w_o = w_out[i].astype(BF16)
        xp = _out_proj(o_p, om_p, xp, w_o, post_norm[i], tm=1024)
        xs = _out_proj(o_s, om_s, xs, w_o, post_norm[i], tm=256)

    return (xp.reshape(batch, seq, d), xs.reshape(dec_batch, t_new, d),
            jnp.transpose(sb_state_p[0], heads_major),
            jnp.transpose(sb_state_p[1], heads_major),
            jnp.stack(sb_ks).reshape(n_a, dec_batch, t_new, SB_HEADS, SB_HD),
            jnp.stack(sb_vs).reshape(n_a, dec_batch, t_new, SB_HEADS, SB_HD),
            jnp.stack(swa_kp).reshape(n_b, batch, WINDOW, SWA_KV_HEADS, SWA_HD),
            jnp.stack(swa_vp).reshape(n_b, batch, WINDOW, SWA_KV_HEADS, SWA_HD),
            jnp.stack(swa_ks).reshape(n_b, dec_batch, WINDOW, SWA_KV_HEADS, SWA_HD),
            jnp.stack(swa_vs).reshape(n_b, dec_batch, WINDOW, SWA_KV_HEADS, SWA_HD),
            mem_k.reshape(depth, batch, MEM_LEN, MEM_HEADS, MEM_HD),
            mem_v.reshape(depth, batch, MEM_LEN, MEM_HEADS, MEM_HD))
```

```python
import functools
import math

import numpy as np
import jax
import jax.numpy as jnp
from jax import lax
from jax.experimental import pallas as pl
from jax.experimental.pallas import tpu as pltpu

F32 = jnp.float32
BF16 = jnp.bfloat16

D_MODEL = 1024
CHUNK = 64
MEM_LEN = 256
MEM_HEADS = 4
MEM_HD = 128
MEM_W = MEM_HEADS * MEM_HD
SB_HEADS = 12
SB_HD = 128
SB_W = SB_HEADS * SB_HD
SWA_HEADS = 24
SWA_KV_HEADS = 3
SWA_GROUP = SWA_HEADS // SWA_KV_HEADS
SWA_HD = 64
SWA_W = SWA_HEADS * SWA_HD
SWA_KV_W = SWA_KV_HEADS * SWA_HD
WINDOW = 128
BAND = WINDOW + CHUNK
NUM_BUCKETS = 32
MAX_DISTANCE = 128
EPS = 1e-6
NEG_INF = -1e30

VMEM_LIMIT_BYTES = 56 * 1024 * 1024
VMEM_LIMIT_FRESH_STATE_BYTES = 60 * 1024 * 1024


def _params(semantics, vmem_limit_bytes=VMEM_LIMIT_BYTES):
    return pltpu.CompilerParams(dimension_semantics=semantics, vmem_limit_bytes=vmem_limit_bytes)


def _resident(shape):
    nd = len(shape)
    return pl.BlockSpec(shape, lambda *_: (0,) * nd, pipeline_mode=pl.Buffered(1))


def _rmsnorm(x, g):
    return x * lax.rsqrt(jnp.mean(x * x, axis=-1, keepdims=True) + EPS) * g


def _dot(a, b):
    return jnp.dot(a, b, preferred_element_type=F32)


def _dot_nt(a, b):
    return lax.dot_general(a, b, (((1,), (1,)), ((), ())), preferred_element_type=F32)


PROJ_COL_CHUNK = 512
PROJ_SUBTILES = 2


def _silu(z):
    return z * (1.0 / (1.0 + jnp.exp(-z)))


def _norm_proj_kernel(x_ref, g_ref, *refs, widths, state_idx, gate_idx, n_prev, slab):
    n = len(widths)
    w_refs = refs[:n]
    o_refs = refs[n + n_prev:2 * n + n_prev]
    s_refs = refs[2 * n + n_prev:]
    s_slab = 0 if n_prev else slab
    rows = x_ref.shape[0] // PROJ_SUBTILES
    subs = [slice(r * rows, (r + 1) * rows) for r in range(PROJ_SUBTILES)]
    h = [_rmsnorm(x_ref[rs, :], g_ref[...]).astype(BF16) for rs in subs]
    for idx, (w_ref, o_ref, width) in enumerate(zip(w_refs, o_refs, widths)):
        s_ref = s_refs[state_idx.index(idx)] if idx in state_idx else None
        for c in range(0, width, PROJ_COL_CHUNK):
            cw = min(PROJ_COL_CHUNK, width - c)
            for rs, hs in zip(subs, h):
                y = _dot(hs, w_ref[:, c:c + cw])
                o_ref[rs, c:c + cw] = (_silu(y) if idx in gate_idx else y).astype(o_ref.dtype)
                if s_ref is not None:
                    for hh in range(cw // SB_HD):
                        s_ref[s_slab, 0, c // SB_HD + hh, rs, :] = y[:, hh * SB_HD:(hh + 1) * SB_HD]
    for s_ref in s_refs:
        for other in range(s_ref.shape[0]):
            if other != s_slab:
                s_ref[other] = jnp.zeros(s_ref.shape[1:], s_ref.dtype)


def _norm_proj(x, gain, weights, dtypes, tm, gate_idx=(), state_idx=(), slab=0, n_slabs=1, seq=None,
               prev_states=None):
    n_tok, d = x.shape
    widths = tuple(w.shape[1] for w in weights)
    prev = tuple(prev_states) if prev_states is not None else ()
    n_in = 2 + len(weights)
    state_specs, state_shapes = [], []
    if state_idx:
        nt = seq // tm
        if prev:
            spec = pl.BlockSpec((1, 1, SB_HEADS, tm, SB_HD), lambda i: (slab, i // nt, 0, i % nt, 0))
        else:
            spec = pl.BlockSpec((n_slabs, 1, SB_HEADS, tm, SB_HD), lambda i: (0, i // nt, 0, i % nt, 0))
        state_specs = [spec for _ in state_idx]
        state_shapes = [jax.ShapeDtypeStruct((n_slabs, n_tok // seq, SB_HEADS, seq, SB_HD), F32)
                        for _ in state_idx]
    outs = pl.pallas_call(
        functools.partial(_norm_proj_kernel, widths=widths, state_idx=tuple(state_idx),
                          gate_idx=tuple(gate_idx), n_prev=len(prev), slab=slab),
        grid=(n_tok // tm,),
        in_specs=[pl.BlockSpec((tm, d), lambda i: (i, 0)), _resident((1, d))]
        + [_resident(w.shape) for w in weights]
        + [pl.BlockSpec(memory_space=pl.ANY) for _ in prev],
        out_specs=[pl.BlockSpec((tm, w), lambda i: (i, 0)) for w in widths] + state_specs,
        out_shape=[jax.ShapeDtypeStruct((n_tok, w), dt) for w, dt in zip(widths, dtypes)] + state_shapes,
        input_output_aliases={n_in + s: len(widths) + s for s in range(len(prev))},
        compiler_params=_params(("parallel",), VMEM_LIMIT_FRESH_STATE_BYTES if state_idx and not prev
                                else VMEM_LIMIT_BYTES),
        name="norm_proj",
    )(x, gain.reshape(1, d), *weights, *prev)
    return outs[:len(widths)], outs[len(widths):]


def _mem_proj_kernel(x_ref, g_ref, w_ref, k_ref, v_ref):
    h = _rmsnorm(x_ref[...], g_ref[0]).astype(BF16)
    k_ref[0] = _dot(h, w_ref[0, :, :MEM_W])
    v_ref[0] = _dot(h, w_ref[0, :, MEM_W:])


def _mem_proj(mem, gains, w, tm):
    n_tok, d = mem.shape
    depth = w.shape[0]
    out = jax.ShapeDtypeStruct((depth, n_tok, MEM_W), F32)
    return pl.pallas_call(
        _mem_proj_kernel,
        grid=(depth, n_tok // tm),
        in_specs=[pl.BlockSpec((tm, d), lambda l, i: (i, 0)),
                  pl.BlockSpec((1, 1, d), lambda l, i: (l, 0, 0)),
                  pl.BlockSpec((1, d, 2 * MEM_W), lambda l, i: (l, 0, 0))],
        out_specs=[pl.BlockSpec((1, tm, MEM_W), lambda l, i: (l, i, 0))] * 2,
        out_shape=[out, out],
        compiler_params=_params(("parallel", "parallel")),
        name="mem_proj",
    )(mem, gains.reshape(depth, 1, d), w)


LOG2E = math.log2(math.e)
SB_LOG2_SCALE = SB_HD ** -0.5 * LOG2E
SB_TQ = 256
SB_TK = 256
SB_HEADS_PER_STEP = 6
SB_SAMPLE_HEADS_PER_STEP = 6
SB_LIVE_ROWS = 160
SB_DEAD_LOG2 = 151.0


def _later_key_matrix(n):
    later = lax.broadcasted_iota(jnp.int32, (n, n), 0) > lax.broadcasted_iota(jnp.int32, (n, n), 1)
    return jnp.where(later, 1.0, 0.0).astype(BF16)


def _strict_causal(n_rows, n_cols, first_row):
    return (lax.broadcasted_iota(jnp.int32, (n_rows, n_cols), 1)
            < lax.broadcasted_iota(jnp.int32, (n_rows, n_cols), 0) + first_row)


def _sb_visit(chains):
    z2 = [_dot_nt(c[0], c[1]) * SB_LOG2_SCALE for c in chains]
    nl = [jnp.maximum(z, 0.0) + jnp.log2(1.0 + jnp.exp2(-jnp.abs(z))) for z in z2]
    nl = [x if c[4] is None else jnp.where(c[4], x, 0.0) for x, c in zip(nl, chains)]
    ncarry = [c[5] + jnp.sum(x, axis=-1, keepdims=True) for x, c in zip(nl, chains)]
    head = [z - x - c[5] for z, x, c in zip(z2, nl, chains)]
    a = [jnp.exp2(t - _dot(x.astype(BF16), c[3])) for t, x, c in zip(head, nl, chains)]
    a = [x if c[4] is None else jnp.where(c[4], x, 0.0) for x, c in zip(a, chains)]
    acc = [c[6] + _dot(x.astype(BF16), c[2]) for x, c in zip(a, chains)]
    return tuple(zip(ncarry, acc))


def _sb_older_blocks(n_blocks, visit, nc_ref, acc_ref, live_rows=None):
    heads, total = nc_ref.shape[0], nc_ref.shape[1]

    def liveness():
        least = functools.reduce(jnp.minimum, [nc_ref[c] for c in range(heads)])
        if live_rows is None:
            return (jnp.min(least) < SB_DEAD_LOG2).astype(jnp.int32), jnp.int32(0)
        lead, tail = jnp.min(least[:live_rows]), jnp.min(least[live_rows:])
        return ((jnp.minimum(lead, tail) < SB_DEAD_LOG2).astype(jnp.int32),
                (tail >= SB_DEAD_LOG2).astype(jnp.int32))

    def step(block, rows):
        new = visit(block, rows, tuple((nc_ref[c, :rows], acc_ref[c, :rows]) for c in range(heads)))
        _sb_store(nc_ref, acc_ref, new, rows)

    def cond(loop):
        return jnp.logical_and(loop[0] < n_blocks, loop[1] > 0)

    def body(loop):
        block, _, tail_dead = loop
        if live_rows is None:
            step(block, total)
        else:
            pl.when(tail_dead > 0)(lambda: step(block, live_rows))
            pl.when(tail_dead == 0)(lambda: step(block, total))
        return (block + 1,) + liveness()

    return lax.while_loop(cond, body, (jnp.int32(0),) + liveness())[0]


def _sb_store(nc_ref, acc_ref, state, rows):
    for c, (ncarry, acc) in enumerate(state):
        nc_ref[c, :rows] = ncarry
        acc_ref[c, :rows] = acc


def _sb_zero(rows):
    return jnp.zeros((rows, 1), F32), jnp.zeros((rows, SB_HD), F32)


def _sb_prompt_kernel(q_ref, k_ref, v_ref, gate_ref, o_ref, nc_ref, acc_ref, *, heads):
    i = pl.program_id(2)
    cols = [slice(c * SB_HD, (c + 1) * SB_HD) for c in range(heads)]
    later = _later_key_matrix(SB_TK)
    diag = pl.ds(pl.multiple_of(i * SB_TK, SB_TK), SB_TK)
    mask = _strict_causal(SB_TQ, SB_TK, 0)
    _sb_store(nc_ref, acc_ref,
              _sb_visit([(q_ref[:, cs], k_ref[diag, cs], v_ref[diag, cs], later, mask) + _sb_zero(SB_TQ)
                         for cs in cols]), SB_TQ)

    def older(jj, rows, st):
        keys = pl.ds(pl.multiple_of((i - 1 - jj) * SB_TK, SB_TK), SB_TK)
        return _sb_visit([(q_ref[:rows, cs], k_ref[keys, cs], v_ref[keys, cs], later, None) + tuple(s)
                          for cs, s in zip(cols, st)])

    _sb_older_blocks(i, older, nc_ref, acc_ref, live_rows=SB_LIVE_ROWS)
    for c, cs in enumerate(cols):
        o_ref[:, cs] = (acc_ref[c] * gate_ref[:, cs]).astype(o_ref.dtype)


def _sb_prompt(q, k, v, gate, batch, seq):
    n_tok = batch * seq
    nq = seq // SB_TQ
    heads = SB_HEADS_PER_STEP
    width = heads * SB_HD
    return pl.pallas_call(
        functools.partial(_sb_prompt_kernel, heads=heads),
        grid=(batch, SB_HEADS // heads, nq),
        in_specs=[pl.BlockSpec((SB_TQ, width), lambda b, h, i: (b * nq + i, h)),
                  pl.BlockSpec((seq, width), lambda b, h, i: (b, h)),
                  pl.BlockSpec((seq, width), lambda b, h, i: (b, h)),
                  pl.BlockSpec((SB_TQ, width), lambda b, h, i: (b * nq + i, h))],
        out_specs=pl.BlockSpec((SB_TQ, width), lambda b, h, i: (b * nq + i, h)),
        out_shape=jax.ShapeDtypeStruct((n_tok, SB_W), BF16),
        scratch_shapes=[pltpu.VMEM((heads, SB_TQ, 1), F32), pltpu.VMEM((heads, SB_TQ, SB_HD), F32)],
        compiler_params=_params(("parallel", "parallel", "parallel")),
        name="sb_prompt",
    )(q, k, v, gate)


def _sb_sample_kernel(q_ref, kn_ref, vn_ref, kc_hbm, vc_hbm, gate_ref, o_ref, nc_ref, acc_ref, kbuf, vbuf, ksem, vsem,
                      *, heads, t_new, past, layer):
    b, hg = pl.program_id(0), pl.program_id(1)
    cols = [slice(c * SB_HD, (c + 1) * SB_HD) for c in range(heads)]
    n_blocks = past // SB_TK

    def copies(block, slot):
        keys = pl.ds(pl.multiple_of((n_blocks - 1 - block) * SB_TK, SB_TK), SB_TK)
        src = lambda ref: ref.at[layer, b, pl.ds(hg * heads, heads), keys, :]
        return (pltpu.make_async_copy(src(kc_hbm), kbuf.at[slot], ksem.at[slot]),
                pltpu.make_async_copy(src(vc_hbm), vbuf.at[slot], vsem.at[slot]))

    def start(block, slot):
        for cp in copies(block, slot):
            cp.start()

    def wait(block, slot):
        for cp in copies(block, slot):
            cp.wait()

    start(0, 0)
    later_new, mask_new = _later_key_matrix(t_new), _strict_causal(t_new, t_new, 0)
    _sb_store(nc_ref, acc_ref,
              _sb_visit([(q_ref[:, cs], kn_ref[:, cs].astype(BF16), vn_ref[:, cs].astype(BF16), later_new, mask_new)
                         + _sb_zero(t_new) for cs in cols]), t_new)
    later = _later_key_matrix(SB_TK)

    def older(jj, rows, st):
        slot = lax.rem(jj, 2)
        wait(jj, slot)
        pl.when(jj + 1 < n_blocks)(lambda: start(jj + 1, 1 - slot))
        return _sb_visit([(q_ref[:, cs], kbuf[slot, c].astype(BF16), vbuf[slot, c].astype(BF16), later, None) + tuple(s)
                          for c, (cs, s) in enumerate(zip(cols, st))])

    visited = _sb_older_blocks(n_blocks, older, nc_ref, acc_ref)
    pl.when(visited < n_blocks)(lambda: wait(visited, lax.rem(visited, 2)))
    for c, cs in enumerate(cols):
        o_ref[:, cs] = (acc_ref[c] * gate_ref[:, cs]).astype(o_ref.dtype)


def _sb_sample(q, k_new, v_new, k_cache, v_cache, gate, layer, t_new):
    _, batch, _, past, _ = k_cache.shape
    heads = SB_SAMPLE_HEADS_PER_STEP
    width = heads * SB_HD
    new_spec = pl.BlockSpec((t_new, width), lambda b, h: (b, h))
    cache_spec = pl.BlockSpec(memory_space=pl.ANY)
    block_buf = pltpu.VMEM((2, heads, SB_TK, SB_HD), F32)
    return pl.pallas_call(
        functools.partial(_sb_sample_kernel, heads=heads, t_new=t_new, past=past, layer=layer),
        grid=(batch, SB_HEADS // heads),
        in_specs=[new_spec, new_spec, new_spec, cache_spec, cache_spec, new_spec],
        out_specs=new_spec,
        out_shape=jax.ShapeDtypeStruct((batch * t_new, SB_W), BF16),
        scratch_shapes=[pltpu.VMEM((heads, t_new, 1), F32), pltpu.VMEM((heads, t_new, SB_HD), F32),
                        block_buf, block_buf, pltpu.SemaphoreType.DMA((2,)), pltpu.SemaphoreType.DMA((2,))],
        compiler_params=_params(("arbitrary", "arbitrary")),
        name="sb_sample",
    )(q, k_new, v_new, k_cache, v_cache, gate)


SWA_SCALE = SWA_HD ** -0.5
SWA_PAIRS = SWA_GROUP // 2
SWA_PAIR_W = 2 * SWA_HD
SWA_COLS = SWA_PAIRS * CHUNK
SWA_BIAS_VARIANTS = 3
SWA_CHUNKS_PER_STEP = 8


def _swa_fill(dst_ref, front_ref, body_ref, seq):
    for g in range(SWA_KV_HEADS):
        for (lo, hi), src_ref in (((0, WINDOW), front_ref), ((WINDOW, WINDOW + seq), body_ref)):
            x = src_ref[:, g * SWA_HD:(g + 1) * SWA_HD]
            zero = jnp.zeros_like(x)
            dst_ref[g, 0, lo:hi, :] = jnp.concatenate([x, zero], axis=1).astype(BF16)
            dst_ref[g, 1, lo:hi, :] = jnp.concatenate([zero, x], axis=1).astype(BF16)


def _swa_softmax(s, sk):
    m = jnp.maximum(jnp.max(s, axis=0, keepdims=True), sk)
    p = jnp.exp(s - m)
    den = jnp.sum(p, axis=0, keepdims=True) + jnp.exp(sk - m)
    return p.astype(BF16), 1.0 / den


def _swa_kernel(q_ref, k_ref, v_ref, kf_ref, vf_ref, bias_ref, sink_ref, gate_ref, o_ref, kab_ref, vab_ref,
                *, chunks, seq, front_valid):
    t = pl.program_id(1)

    @pl.when(t == 0)
    def _():
        _swa_fill(kab_ref, kf_ref, k_ref, seq)
        _swa_fill(vab_ref, vf_ref, v_ref, seq)

    blocks = [(c, g) for c in range(chunks) for g in range(SWA_KV_HEADS)]

    def band(ref, c, g):
        rows = pl.ds(pl.multiple_of((t * chunks + c) * CHUNK, CHUNK), BAND)
        return jnp.concatenate([ref[g, 0, rows, :], ref[g, 1, rows, :]], axis=0)

    def q_pairs(c, g):
        return jnp.concatenate(
            [q_ref[c * CHUNK:(c + 1) * CHUNK, (g * SWA_PAIRS + r) * SWA_PAIR_W:(g * SWA_PAIRS + r + 1) * SWA_PAIR_W]
             for r in range(SWA_PAIRS)], axis=0)

    def variant(c):
        return SWA_BIAS_VARIANTS - 1 if front_valid else jnp.minimum(t * chunks + c, SWA_BIAS_VARIANTS - 1)

    s = [_dot_nt(band(kab_ref, c, g), q_pairs(c, g)) * SWA_SCALE + bias_ref[variant(c), g] for c, g in blocks]
    p = [(_swa_softmax(x[:BAND], sink_ref[g, 0]), _swa_softmax(x[BAND:], sink_ref[g, 1])) for x, (c, g) in zip(s, blocks)]
    pv = [lax.dot_general(band(vab_ref, c, g), jnp.concatenate([even[0], odd[0]], axis=0),
                          (((0,), (0,)), ((), ())), preferred_element_type=F32)
          for (even, odd), (c, g) in zip(p, blocks)]
    o = [jnp.concatenate([x[:SWA_HD] * even[1], x[SWA_HD:] * odd[1]], axis=0).T
         for x, (even, odd) in zip(pv, p)]
    for x, (c, g) in zip(o, blocks):
        for r in range(SWA_PAIRS):
            lanes = slice((g * SWA_PAIRS + r) * SWA_PAIR_W, (g * SWA_PAIRS + r + 1) * SWA_PAIR_W)
            rows = slice(c * CHUNK, (c + 1) * CHUNK)
            o_ref[rows, lanes] = (x[r * CHUNK:(r + 1) * CHUNK] * gate_ref[rows, lanes]).astype(o_ref.dtype)


def _swa(q, k, v, k_front, v_front, bias, sink, gate, batch, seq, front_valid):
    chunks = min(SWA_CHUNKS_PER_STEP, seq // CHUNK)
    tile = chunks * CHUNK
    nt = seq // tile
    kv_spec = pl.BlockSpec((seq, SWA_KV_W), lambda b, t: (b, 0))
    front_spec = pl.BlockSpec((WINDOW, SWA_KV_W), lambda b, t: (b, 0))
    return pl.pallas_call(
        functools.partial(_swa_kernel, chunks=chunks, seq=seq, front_valid=front_valid),
        grid=(batch, nt),
        in_specs=[pl.BlockSpec((tile, SWA_W), lambda b, t: (b * nt + t, 0)),
                  kv_spec, kv_spec, front_spec, front_spec,
                  _resident(bias.shape), _resident(sink.shape),
                  pl.BlockSpec((tile, SWA_W), lambda b, t: (b * nt + t, 0))],
        out_specs=pl.BlockSpec((tile, SWA_W), lambda b, t: (b * nt + t, 0)),
        out_shape=jax.ShapeDtypeStruct((batch * seq, SWA_W), BF16),
        scratch_shapes=[pltpu.VMEM((SWA_KV_HEADS, 2, seq + WINDOW, SWA_PAIR_W), BF16)] * 2,
        compiler_params=_params(("parallel", "arbitrary")),
        name="swa",
    )(q, k, v, k_front, v_front, bias, sink, gate)


def _t5_bucket(rel):
    nb = NUM_BUCKETS // 2
    max_exact = nb // 2
    ret = np.where(rel > 0, nb, 0)
    n = np.abs(rel)
    nf = np.maximum(n, 1).astype(np.float32)
    large = max_exact + (np.log(nf / np.float32(max_exact)) / np.float32(math.log(MAX_DISTANCE / max_exact))
                         * np.float32(nb - max_exact)).astype(np.int32)
    large = np.minimum(large, nb - 1)
    return ret + np.where(n < max_exact, n, large)


def _band_bias(rel_bias):
    rel = np.arange(BAND)[None, :] - WINDOW - np.arange(CHUNK)[:, None]
    onehot = np.eye(NUM_BUCKETS, dtype=np.float32)[_t5_bucket(rel).reshape(-1)]
    b = jnp.dot(jnp.asarray(onehot), rel_bias.astype(F32), precision=lax.Precision.HIGHEST)
    b = b.reshape(CHUNK, BAND, SWA_KV_HEADS, SWA_PAIRS, 2)
    b = jnp.transpose(b, (2, 4, 1, 3, 0)).reshape(SWA_KV_HEADS, 2 * BAND, SWA_COLS)
    first_valid = np.array([WINDOW - v * CHUNK for v in range(SWA_BIAS_VARIANTS)])
    key = np.tile(np.arange(BAND), 2)
    mask = np.where(key[None, :] < first_valid[:, None], NEG_INF, 0.0).astype(np.float32)
    return b[None] + jnp.asarray(mask)[:, None, :, None]


def _sink_cols(sink):
    s = jnp.transpose(sink.astype(F32).reshape(SWA_KV_HEADS, SWA_PAIRS, 2), (0, 2, 1))
    return jnp.repeat(s, CHUNK, axis=-1).reshape(SWA_KV_HEADS, 2, 1, SWA_COLS)


MEM_SCALE = MEM_HD ** -0.5


def _mem_attn_kernel(q_ref, k_ref, v_ref, gate_ref, o_ref):
    cols = [slice(h * MEM_HD, (h + 1) * MEM_HD) for h in range(MEM_HEADS)]
    ones = jnp.ones((MEM_LEN, MEM_HD), BF16)
    s = [_dot_nt(q_ref[:, cs], k_ref[0, :, cs].astype(BF16)) * MEM_SCALE for cs in cols]
    p = [jnp.exp(x - jnp.max(x, axis=-1, keepdims=True)).astype(BF16) for x in s]
    pv = [_dot(x, jnp.concatenate([v_ref[0, :, cs].astype(BF16), ones], axis=1)) for x, cs in zip(p, cols)]
    for x, cs in zip(pv, cols):
        o_ref[:, cs] = (x[:, :MEM_HD] * (1.0 / x[:, MEM_HD:]) * gate_ref[:, cs]).astype(o_ref.dtype)


def _mem_attn(qm, mk, mv, gate, layer, batch, seq, tq):
    nt = seq // tq
    gate_block = gate.shape[1] // MEM_W - 1
    kv_spec = pl.BlockSpec((1, MEM_LEN, MEM_W), lambda b, t: (layer, b, 0))
    return pl.pallas_call(
        _mem_attn_kernel,
        grid=(batch, nt),
        in_specs=[pl.BlockSpec((tq, MEM_W), lambda b, t: (b * nt + t, 0)), kv_spec, kv_spec,
                  pl.BlockSpec((tq, MEM_W), lambda b, t: (b * nt + t, gate_block))],
        out_specs=pl.BlockSpec((tq, MEM_W), lambda b, t: (b * nt + t, 0)),
        out_shape=jax.ShapeDtypeStruct((batch * seq, MEM_W), BF16),
        compiler_params=_params(("parallel", "parallel")),
        name="mem_attn",
    )(qm, mk, mv, gate)


OUT_PROJ_SUBTILES = 2


def _out_proj_kernel(u_ref, um_ref, x_ref, w_ref, g_ref, y_ref, *, mix_w):
    rows = u_ref.shape[0] // OUT_PROJ_SUBTILES
    subs = [slice(r * rows, (r + 1) * rows) for r in range(OUT_PROJ_SUBTILES)]
    y = [_dot(u_ref[rs, :], w_ref[:mix_w, :]) + _dot(um_ref[rs, :], w_ref[mix_w:, :]) for rs in subs]
    for rs, yy in zip(subs, y):
        y_ref[rs, :] = x_ref[rs, :] + _rmsnorm(yy, g_ref[...])


def _out_proj(u, um, x, w, gain, tm):
    n_tok, d = x.shape
    mix_w = u.shape[1]
    row = lambda width: pl.BlockSpec((tm, width), lambda i: (i, 0))
    return pl.pallas_call(
        functools.partial(_out_proj_kernel, mix_w=mix_w),
        grid=(n_tok // tm,),
        in_specs=[row(mix_w), row(MEM_W), row(d), _resident(w.shape), _resident((1, d))],
        out_specs=row(d),
        out_shape=jax.ShapeDtypeStruct((n_tok, d), F32),
        compiler_params=_params(("parallel",)),
        name="out_proj",
    )(u, um, x, w, gain.reshape(1, d))


def _split_cols(w, widths):
    out, off = [], 0
    for width in widths:
        out.append(w[:, off:off + width].astype(BF16))
        off += width
    return out


@jax.jit
def kernel(x_prompt, x_sample, mem_prompt, cache_sb_k, cache_sb_v, cache_swa_k, cache_swa_v,
           cache_mem_k, cache_mem_v, pre_norm, post_norm, mem_norm, w_in_a, w_in_b, w_mem_kv,
           w_out, rel_bias, sinks):
    batch, seq, d = x_prompt.shape
    dec_batch, t_new, _ = x_sample.shape
    depth = pre_norm.shape[0]
    past = cache_sb_k.shape[2]
    n_p, n_s = batch * seq, dec_batch * t_new
    d_inner = w_out.shape[1]

    xp = x_prompt.reshape(n_p, d)
    xs = x_sample.reshape(n_s, d)
    proj_dtypes = (BF16, F32, F32, BF16, BF16)
    gate = (4,)
    widths_a = (SB_W, SB_W, SB_W, MEM_W, d_inner)
    widths_b = (SWA_W, SWA_KV_W, SWA_KV_W, MEM_W, d_inner)

    mem_k, mem_v = _mem_proj(mem_prompt.reshape(batch * MEM_LEN, d), mem_norm,
                             w_mem_kv.astype(BF16), tm=512)
    bias = _band_bias(rel_bias)
    zero_front = jnp.zeros((batch * WINDOW, SWA_KV_W), F32)

    n_a, n_b = (depth + 1) // 2, depth // 2
    sb_state_p = None
    heads_major = (0, 1, 3, 2, 4)
    cache_k = jnp.transpose(cache_sb_k, heads_major)
    cache_v = jnp.transpose(cache_sb_v, heads_major)
    cache_mk = cache_mem_k.reshape(depth, dec_batch * MEM_LEN, MEM_W)
    cache_mv = cache_mem_v.reshape(depth, dec_batch * MEM_LEN, MEM_W)
    sb_ks, sb_vs = [], []
    swa_kp, swa_vp, swa_ks, swa_vs = [], [], [], []
    for i in range(depth):
        j = i // 2
        if i % 2 == 0:
            weights = _split_cols(w_in_a[j], widths_a)
            (q_p, k_p, v_p, qm_p, z_p), sb_state_p = _norm_proj(
                xp, pre_norm[i], weights, (BF16, BF16, BF16, BF16, BF16), tm=512, gate_idx=gate,
                state_idx=(1, 2), slab=j, n_slabs=n_a, seq=seq, prev_states=sb_state_p)
            (q_s, k_s, v_s, qm_s, z_s), _ = _norm_proj(xs, pre_norm[i], weights, proj_dtypes, tm=256,
                                                       gate_idx=gate)
            o_p = _sb_prompt(q_p, k_p, v_p, z_p, batch, seq)
            o_s = _sb_sample(q_s, k_s, v_s, cache_k, cache_v, z_s, j, t_new)
            sb_ks.append(k_s)
            sb_vs.append(v_s)
        else:
            weights = _split_cols(w_in_b[j], widths_b)
            (q_p, k_p, v_p, qm_p, z_p), _ = _norm_proj(xp, pre_norm[i], weights, proj_dtypes, tm=512,
                                                       gate_idx=gate)
            (q_s, k_s, v_s, qm_s, z_s), _ = _norm_proj(xs, pre_norm[i], weights, proj_dtypes, tm=256,
                                                       gate_idx=gate)
            sink = _sink_cols(sinks[j])
            o_p = _swa(q_p, k_p, v_p, zero_front, zero_front, bias, sink, z_p, batch, seq, front_valid=False)
            kc = cache_swa_k[j].reshape(dec_batch * WINDOW, SWA_KV_W)
            vc = cache_swa_v[j].reshape(dec_batch * WINDOW, SWA_KV_W)
            o_s = _swa(q_s, k_s, v_s, kc, vc, bias, sink, z_s, dec_batch, t_new, front_valid=True)
            swa_kp.append(k_p.reshape(batch, seq, SWA_KV_W)[:, seq - WINDOW:])
            swa_vp.append(v_p.reshape(batch, seq, SWA_KV_W)[:, seq - WINDOW:])
            k_all = jnp.concatenate([kc.reshape(dec_batch, WINDOW, SWA_KV_W),
                                     k_s.reshape(dec_batch, t_new, SWA_KV_W)], axis=1)
            v_all = jnp.concatenate([vc.reshape(dec_batch, WINDOW, SWA_KV_W),
                                     v_s.reshape(dec_batch, t_new, SWA_KV_W)], axis=1)
            swa_ks.append(k_all[:, t_new:])
            swa_vs.append(v_all[:, t_new:])
        om_p = _mem_attn(qm_p, mem_k, mem_v, z_p, i, batch, seq, tq=1024)
        om_s = _mem_attn(qm_s, cache_mk, cache_mv, z_s, i, dec_batch, t_new, tq=t_new)
        w_o = w_out[i].astype(BF16)
        xp = _out_proj(o_p, om_p, xp, w_o, post_norm[i], tm=1024)
        xs = _out_proj(o_s, om_s, xs, w_o, post_norm[i], tm=256)

    return (xp.reshape(batch, seq, d), xs.reshape(dec_batch, t_new, d),
            jnp.transpose(sb_state_p[0], heads_major),
            jnp.transpose(sb_state_p[1], heads_major),
            jnp.stack(sb_ks).reshape(n_a, dec_batch, t_new, SB_HEADS, SB_HD),
            jnp.stack(sb_vs).reshape(n_a, dec_batch, t_new, SB_HEADS, SB_HD),
            jnp.stack(swa_kp).reshape(n_b, batch, WINDOW, SWA_KV_HEADS, SWA_HD),
            jnp.stack(swa_vp).reshape(n_b, batch, WINDOW, SWA_KV_HEADS, SWA_HD),
            jnp.stack(swa_ks).reshape(n_b, dec_batch, WINDOW, SWA_KV_HEADS, SWA_HD),
            jnp.stack(swa_vs).reshape(n_b, dec_batch, WINDOW, SWA_KV_HEADS, SWA_HD),
            mem_k.reshape(depth, batch, MEM_LEN, MEM_HEADS, MEM_HD),
            mem_v.reshape(depth, batch, MEM_LEN, MEM_HEADS, MEM_HD))
```

```python
import functools
import math

import numpy as np
import jax
import jax.numpy as jnp
from jax import lax
from jax.experimental import pallas as pl
from jax.experimental.pallas import tpu as pltpu

F32 = jnp.float32
BF16 = jnp.bfloat16

D_MODEL = 1024
CHUNK = 64
MEM_LEN = 256
MEM_HEADS = 4
MEM_HD = 128
MEM_W = MEM_HEADS * MEM_HD
SB_HEADS = 12
SB_HD = 128
SB_W = SB_HEADS * SB_HD
SWA_HEADS = 24
SWA_KV_HEADS = 3
SWA_GROUP = SWA_HEADS // SWA_KV_HEADS
SWA_HD = 64
SWA_W = SWA_HEADS * SWA_HD
SWA_KV_W = SWA_KV_HEADS * SWA_HD
WINDOW = 128
BAND = WINDOW + CHUNK
NUM_BUCKETS = 32
MAX_DISTANCE = 128
EPS = 1e-6
NEG_INF = -1e30

VMEM_LIMIT_BYTES = 56 * 1024 * 1024
VMEM_LIMIT_FRESH_STATE_BYTES = 60 * 1024 * 1024


def _params(semantics, vmem_limit_bytes=VMEM_LIMIT_BYTES):
    return pltpu.CompilerParams(dimension_semantics=semantics, vmem_limit_bytes=vmem_limit_bytes)


def _resident(shape):
    nd = len(shape)
    return pl.BlockSpec(shape, lambda *_: (0,) * nd, pipeline_mode=pl.Buffered(1))


def _rmsnorm(x, g):
    return x * lax.rsqrt(jnp.mean(x * x, axis=-1, keepdims=True) + EPS) * g


def _dot(a, b):
    return jnp.dot(a, b, preferred_element_type=F32)


def _dot_nt(a, b):
    return lax.dot_general(a, b, (((1,), (1,)), ((), ())), preferred_element_type=F32)


PROJ_COL_CHUNK = 512
PROJ_SUBTILES = 2


def _silu(z):
    return z * (1.0 / (1.0 + jnp.exp(-z)))


def _norm_proj_kernel(x_ref, g_ref, *refs, widths, state_idx, gate_idx, n_prev, slab):
    n = len(widths)
    w_refs = refs[:n]
    o_refs = refs[n + n_prev:2 * n + n_prev]
    s_refs = refs[2 * n + n_prev:]
    s_slab = 0 if n_prev else slab
    rows = x_ref.shape[0] // PROJ_SUBTILES
    subs = [slice(r * rows, (r + 1) * rows) for r in range(PROJ_SUBTILES)]
    h = [_rmsnorm(x_ref[rs, :], g_ref[...]).astype(BF16) for rs in subs]
    for idx, (w_ref, o_ref, width) in enumerate(zip(w_refs, o_refs, widths)):
        s_ref = s_refs[state_idx.index(idx)] if idx in state_idx else None
        for c in range(0, width, PROJ_COL_CHUNK):
            cw = min(PROJ_COL_CHUNK, width - c)
            for rs, hs in zip(subs, h):
                y = _dot(hs, w_ref[:, c:c + cw])
                o_ref[rs, c:c + cw] = (_silu(y) if idx in gate_idx else y).astype(o_ref.dtype)
                if s_ref is not None:
                    for hh in range(cw // SB_HD):
                        s_ref[s_slab, 0, c // SB_HD + hh, rs, :] = y[:, hh * SB_HD:(hh + 1) * SB_HD]
    for s_ref in s_refs:
        for other in range(s_ref.shape[0]):
            if other != s_slab:
                s_ref[other] = jnp.zeros(s_ref.shape[1:], s_ref.dtype)


def _norm_proj(x, gain, weights, dtypes, tm, gate_idx=(), state_idx=(), slab=0, n_slabs=1, seq=None,
               prev_states=None):
    n_tok, d = x.shape
    widths = tuple(w.shape[1] for w in weights)
    prev = tuple(prev_states) if prev_states is not None else ()
    n_in = 2 + len(weights)
    state_specs, state_shapes = [], []
    if state_idx:
        nt = seq // tm
        if prev:
            spec = pl.BlockSpec((1, 1, SB_HEADS, tm, SB_HD), lambda i: (slab, i // nt, 0, i % nt, 0))
        else:
            spec = pl.BlockSpec((n_slabs, 1, SB_HEADS, tm, SB_HD), lambda i: (0, i // nt, 0, i % nt, 0))
        state_specs = [spec for _ in state_idx]
        state_shapes = [jax.ShapeDtypeStruct((n_slabs, n_tok // seq, SB_HEADS, seq, SB_HD), F32)
                        for _ in state_idx]
    outs = pl.pallas_call(
        functools.partial(_norm_proj_kernel, widths=widths, state_idx=tuple(state_idx),
                          gate_idx=tuple(gate_idx), n_prev=len(prev), slab=slab),
        grid=(n_tok // tm,),
        in_specs=[pl.BlockSpec((tm, d), lambda i: (i, 0)), _resident((1, d))]
        + [_resident(w.shape) for w in weights]
        + [pl.BlockSpec(memory_space=pl.ANY) for _ in prev],
        out_specs=[pl.BlockSpec((tm, w), lambda i: (i, 0)) for w in widths] + state_specs,
        out_shape=[jax.ShapeDtypeStruct((n_tok, w), dt) for w, dt in zip(widths, dtypes)] + state_shapes,
        input_output_aliases={n_in + s: len(widths) + s for s in range(len(prev))},
        compiler_params=_params(("parallel",), VMEM_LIMIT_FRESH_STATE_BYTES if state_idx and not prev
                                else VMEM_LIMIT_BYTES),
        name="norm_proj",
    )(x, gain.reshape(1, d), *weights, *prev)
    return outs[:len(widths)], outs[len(widths):]


def _mem_proj_kernel(x_ref, g_ref, w_ref, k_ref, v_ref):
    h = _rmsnorm(x_ref[...], g_ref[0]).astype(BF16)
    k_ref[0] = _dot(h, w_ref[0, :, :MEM_W])
    v_ref[0] = _dot(h, w_ref[0, :, MEM_W:])


def _mem_proj(mem, gains, w, tm):
    n_tok, d = mem.shape
    depth = w.shape[0]
    out = jax.ShapeDtypeStruct((depth, n_tok, MEM_W), F32)
    return pl.pallas_call(
        _mem_proj_kernel,
        grid=(depth, n_tok // tm),
        in_specs=[pl.BlockSpec((tm, d), lambda l, i: (i, 0)),
                  pl.BlockSpec((1, 1, d), lambda l, i: (l, 0, 0)),
                  pl.BlockSpec((1, d, 2 * MEM_W), lambda l, i: (l, 0, 0))],
        out_specs=[pl.BlockSpec((1, tm, MEM_W), lambda l, i: (l, i, 0))] * 2,
        out_shape=[out, out],
        compiler_params=_params(("parallel", "parallel")),
        name="mem_proj",
    )(mem, gains.reshape(depth, 1, d), w)


LOG2E = math.log2(math.e)
SB_LOG2_SCALE = SB_HD ** -0.5 * LOG2E
SB_TQ = 256
SB_TK = 256
SB_HEADS_PER_STEP = 6
SB_SAMPLE_HEADS_PER_STEP = 6
SB_QBLOCKS_PER_STEP = 2
SB_LIVE_ROWS = 160
SB_DEAD_LOG2 = 151.0


def _later_key_matrix(n):
    later = lax.broadcasted_iota(jnp.int32, (n, n), 0) > lax.broadcasted_iota(jnp.int32, (n, n), 1)
    return jnp.where(later, 1.0, 0.0).astype(BF16)


def _strict_causal(n_rows, n_cols, first_row):
    return (lax.broadcasted_iota(jnp.int32, (n_rows, n_cols), 1)
            < lax.broadcasted_iota(jnp.int32, (n_rows, n_cols), 0) + first_row)


def _sb_visit(chains):
    z2 = [_dot_nt(c[0], c[1]) * SB_LOG2_SCALE for c in chains]
    nl = [jnp.maximum(z, 0.0) + jnp.log2(1.0 + jnp.exp2(-jnp.abs(z))) for z in z2]
    nl = [x if c[4] is None else jnp.where(c[4], x, 0.0) for x, c in zip(nl, chains)]
    ncarry = [c[5] + jnp.sum(x, axis=-1, keepdims=True) for x, c in zip(nl, chains)]
    head = [z - x - c[5] for z, x, c in zip(z2, nl, chains)]
    a = [jnp.exp2(t - _dot(x.astype(BF16), c[3])) for t, x, c in zip(head, nl, chains)]
    a = [x if c[4] is None else jnp.where(c[4], x, 0.0) for x, c in zip(a, chains)]
    acc = [c[6] + _dot(x.astype(BF16), c[2]) for x, c in zip(a, chains)]
    return tuple(zip(ncarry, acc))


def _sb_older_blocks(n_blocks, visit, nc_ref, acc_ref, live_rows=None):
    heads, total = nc_ref.shape[0], nc_ref.shape[1]

    def liveness():
        least = functools.reduce(jnp.minimum, [nc_ref[c] for c in range(heads)])
        if live_rows is None:
            return (jnp.min(least) < SB_DEAD_LOG2).astype(jnp.int32), jnp.int32(0)
        lead, tail = jnp.min(least[:live_rows]), jnp.min(least[live_rows:])
        return ((jnp.minimum(lead, tail) < SB_DEAD_LOG2).astype(jnp.int32),
                (tail >= SB_DEAD_LOG2).astype(jnp.int32))

    def step(block, rows):
        new = visit(block, rows, tuple((nc_ref[c, :rows], acc_ref[c, :rows]) for c in range(heads)))
        _sb_store(nc_ref, acc_ref, new, rows)

    def cond(loop):
        return jnp.logical_and(loop[0] < n_blocks, loop[1] > 0)

    def body(loop):
        block, _, tail_dead = loop
        if live_rows is None:
            step(block, total)
        else:
            pl.when(tail_dead > 0)(lambda: step(block, live_rows))
            pl.when(tail_dead == 0)(lambda: step(block, total))
        return (block + 1,) + liveness()

    return lax.while_loop(cond, body, (jnp.int32(0),) + liveness())[0]


def _sb_store(nc_ref, acc_ref, state, rows):
    for c, (ncarry, acc) in enumerate(state):
        nc_ref[c, :rows] = ncarry
        acc_ref[c, :rows] = acc


def _sb_zero(rows):
    return jnp.zeros((rows, 1), F32), jnp.zeros((rows, SB_HD), F32)


def _sb_prompt_kernel(q_ref, k_ref, v_ref, gate_ref, o_ref, nc_ref, acc_ref, *, heads):
    cols = [slice(c * SB_HD, (c + 1) * SB_HD) for c in range(heads)]
    later = _later_key_matrix(SB_TK)
    mask = _strict_causal(SB_TQ, SB_TK, 0)
    for sub in range(SB_QBLOCKS_PER_STEP):
        _sb_prompt_block(q_ref, k_ref, v_ref, gate_ref, o_ref, nc_ref, acc_ref, cols, later, mask,
                         pl.program_id(2) * SB_QBLOCKS_PER_STEP + sub, sub * SB_TQ)


def _sb_prompt_block(q_ref, k_ref, v_ref, gate_ref, o_ref, nc_ref, acc_ref, cols, later, mask, i, row0):
    diag = pl.ds(pl.multiple_of(i * SB_TK, SB_TK), SB_TK)
    _sb_store(nc_ref, acc_ref,
              _sb_visit([(q_ref[row0:row0 + SB_TQ, cs], k_ref[diag, cs], v_ref[diag, cs], later, mask)
                         + _sb_zero(SB_TQ) for cs in cols]), SB_TQ)

    def older(jj, rows, st):
        keys = pl.ds(pl.multiple_of((i - 1 - jj) * SB_TK, SB_TK), SB_TK)
        return _sb_visit([(q_ref[row0:row0 + rows, cs], k_ref[keys, cs], v_ref[keys, cs], later, None) + tuple(s)
                          for cs, s in zip(cols, st)])

    _sb_older_blocks(i, older, nc_ref, acc_ref, live_rows=SB_LIVE_ROWS)
    rows = slice(row0, row0 + SB_TQ)
    for c, cs in enumerate(cols):
        o_ref[rows, cs] = (acc_ref[c] * gate_ref[rows, cs]).astype(o_ref.dtype)


def _sb_prompt(q, k, v, gate, batch, seq):
    n_tok = batch * seq
    rows = SB_TQ * SB_QBLOCKS_PER_STEP
    nq = seq // rows
    heads = SB_HEADS_PER_STEP
    width = heads * SB_HD
    return pl.pallas_call(
        functools.partial(_sb_prompt_kernel, heads=heads),
        grid=(batch, SB_HEADS // heads, nq),
        in_specs=[pl.BlockSpec((rows, width), lambda b, h, i: (b * nq + i, h)),
                  pl.BlockSpec((seq, width), lambda b, h, i: (b, h)),
                  pl.BlockSpec((seq, width), lambda b, h, i: (b, h)),
                  pl.BlockSpec((rows, width), lambda b, h, i: (b * nq + i, h))],
        out_specs=pl.BlockSpec((rows, width), lambda b, h, i: (b * nq + i, h)),
        out_shape=jax.ShapeDtypeStruct((n_tok, SB_W), BF16),
        scratch_shapes=[pltpu.VMEM((heads, SB_TQ, 1), F32), pltpu.VMEM((heads, SB_TQ, SB_HD), F32)],
        compiler_params=_params(("parallel", "parallel", "parallel")),
        name="sb_prompt",
    )(q, k, v, gate)


def _sb_sample_kernel(q_ref, kn_ref, vn_ref, kc_hbm, vc_hbm, gate_ref, o_ref, nc_ref, acc_ref, kbuf, vbuf, ksem, vsem,
                      *, heads, t_new, past, layer):
    b, hg = pl.program_id(0), pl.program_id(1)
    cols = [slice(c * SB_HD, (c + 1) * SB_HD) for c in range(heads)]
    n_blocks = past // SB_TK

    def copies(block, slot):
        keys = pl.ds(pl.multiple_of((n_blocks - 1 - block) * SB_TK, SB_TK), SB_TK)
        src = lambda ref: ref.at[layer, b, pl.ds(hg * heads, heads), keys, :]
        return (pltpu.make_async_copy(src(kc_hbm), kbuf.at[slot], ksem.at[slot]),
                pltpu.make_async_copy(src(vc_hbm), vbuf.at[slot], vsem.at[slot]))

    def start(block, slot):
        for cp in copies(block, slot):
            cp.start()

    def wait(block, slot):
        for cp in copies(block, slot):
            cp.wait()

    start(0, 0)
    later_new, mask_new = _later_key_matrix(t_new), _strict_causal(t_new, t_new, 0)
    _sb_store(nc_ref, acc_ref,
              _sb_visit([(q_ref[:, cs], kn_ref[:, cs].astype(BF16), vn_ref[:, cs].astype(BF16), later_new, mask_new)
                         + _sb_zero(t_new) for cs in cols]), t_new)
    later = _later_key_matrix(SB_TK)

    def older(jj, rows, st):
        slot = lax.rem(jj, 2)
        wait(jj, slot)
        pl.when(jj + 1 < n_blocks)(lambda: start(jj + 1, 1 - slot))
        return _sb_visit([(q_ref[:, cs], kbuf[slot, c].astype(BF16), vbuf[slot, c].astype(BF16), later, None) + tuple(s)
                          for c, (cs, s) in enumerate(zip(cols, st))])

    visited = _sb_older_blocks(n_blocks, older, nc_ref, acc_ref)
    pl.when(visited < n_blocks)(lambda: wait(visited, lax.rem(visited, 2)))
    for c, cs in enumerate(cols):
        o_ref[:, cs] = (acc_ref[c] * gate_ref[:, cs]).astype(o_ref.dtype)


def _sb_sample(q, k_new, v_new, k_cache, v_cache, gate, layer, t_new):
    _, batch, _, past, _ = k_cache.shape
    heads = SB_SAMPLE_HEADS_PER_STEP
    width = heads * SB_HD
    new_spec = pl.BlockSpec((t_new, width), lambda b, h: (b, h))
    cache_spec = pl.BlockSpec(memory_space=pl.ANY)
    block_buf = pltpu.VMEM((2, heads, SB_TK, SB_HD), F32)
    return pl.pallas_call(
        functools.partial(_sb_sample_kernel, heads=heads, t_new=t_new, past=past, layer=layer),
        grid=(batch, SB_HEADS // heads),
        in_specs=[new_spec, new_spec, new_spec, cache_spec, cache_spec, new_spec],
        out_specs=new_spec,
        out_shape=jax.ShapeDtypeStruct((batch * t_new, SB_W), BF16),
        scratch_shapes=[pltpu.VMEM((heads, t_new, 1), F32), pltpu.VMEM((heads, t_new, SB_HD), F32),
                        block_buf, block_buf, pltpu.SemaphoreType.DMA((2,)), pltpu.SemaphoreType.DMA((2,))],
        compiler_params=_params(("arbitrary", "arbitrary")),
        name="sb_sample",
    )(q, k_new, v_new, k_cache, v_cache, gate)


SWA_SCALE = SWA_HD ** -0.5
SWA_PAIRS = SWA_GROUP // 2
SWA_PAIR_W = 2 * SWA_HD
SWA_COLS = SWA_PAIRS * CHUNK
SWA_BIAS_VARIANTS = 3
SWA_CHUNKS_PER_STEP = 8


def _swa_fill(dst_ref, front_ref, body_ref, seq):
    for g in range(SWA_KV_HEADS):
        for (lo, hi), src_ref in (((0, WINDOW), front_ref), ((WINDOW, WINDOW + seq), body_ref)):
            x = src_ref[:, g * SWA_HD:(g + 1) * SWA_HD]
            zero = jnp.zeros_like(x)
            dst_ref[g, 0, lo:hi, :] = jnp.concatenate([x, zero], axis=1).astype(BF16)
            dst_ref[g, 1, lo:hi, :] = jnp.concatenate([zero, x], axis=1).astype(BF16)


def _swa_softmax(s, sk):
    m = jnp.maximum(jnp.max(s, axis=0, keepdims=True), sk)
    p = jnp.exp(s - m)
    den = jnp.sum(p, axis=0, keepdims=True) + jnp.exp(sk - m)
    return p.astype(BF16), 1.0 / den


def _swa_kernel(q_ref, k_ref, v_ref, kf_ref, vf_ref, bias_ref, sink_ref, gate_ref, o_ref, kab_ref, vab_ref,
                *, chunks, seq, front_valid):
    t = pl.program_id(1)

    @pl.when(t == 0)
    def _():
        _swa_fill(kab_ref, kf_ref, k_ref, seq)
        _swa_fill(vab_ref, vf_ref, v_ref, seq)

    blocks = [(c, g) for c in range(chunks) for g in range(SWA_KV_HEADS)]

    def band(ref, c, g):
        rows = pl.ds(pl.multiple_of((t * chunks + c) * CHUNK, CHUNK), BAND)
        return jnp.concatenate([ref[g, 0, rows, :], ref[g, 1, rows, :]], axis=0)

    def q_pairs(c, g):
        return jnp.concatenate(
            [q_ref[c * CHUNK:(c + 1) * CHUNK, (g * SWA_PAIRS + r) * SWA_PAIR_W:(g * SWA_PAIRS + r + 1) * SWA_PAIR_W]
             for r in range(SWA_PAIRS)], axis=0)

    def variant(c):
        return SWA_BIAS_VARIANTS - 1 if front_valid else jnp.minimum(t * chunks + c, SWA_BIAS_VARIANTS - 1)

    s = [_dot_nt(band(kab_ref, c, g), q_pairs(c, g)) * SWA_SCALE + bias_ref[variant(c), g] for c, g in blocks]
    p = [(_swa_softmax(x[:BAND], sink_ref[g, 0]), _swa_softmax(x[BAND:], sink_ref[g, 1])) for x, (c, g) in zip(s, blocks)]
    pv = [lax.dot_general(band(vab_ref, c, g), jnp.concatenate([even[0], odd[0]], axis=0),
                          (((0,), (0,)), ((), ())), preferred_element_type=F32)
          for (even, odd), (c, g) in zip(p, blocks)]
    o = [jnp.concatenate([x[:SWA_HD] * even[1], x[SWA_HD:] * odd[1]], axis=0).T
         for x, (even, odd) in zip(pv, p)]
    for x, (c, g) in zip(o, blocks):
        for r in range(SWA_PAIRS):
            lanes = slice((g * SWA_PAIRS + r) * SWA_PAIR_W, (g * SWA_PAIRS + r + 1) * SWA_PAIR_W)
            rows = slice(c * CHUNK, (c + 1) * CHUNK)
            o_ref[rows, lanes] = (x[r * CHUNK:(r + 1) * CHUNK] * gate_ref[rows, lanes]).astype(o_ref.dtype)


def _swa(q, k, v, k_front, v_front, bias, sink, gate, batch, seq, front_valid):
    chunks = min(SWA_CHUNKS_PER_STEP, seq // CHUNK)
    tile = chunks * CHUNK
    nt = seq // tile
    kv_spec = pl.BlockSpec((seq, SWA_KV_W), lambda b, t: (b, 0))
    front_spec = pl.BlockSpec((WINDOW, SWA_KV_W), lambda b, t: (b, 0))
    return pl.pallas_call(
        functools.partial(_swa_kernel, chunks=chunks, seq=seq, front_valid=front_valid),
        grid=(batch, nt),
        in_specs=[pl.BlockSpec((tile, SWA_W), lambda b, t: (b * nt + t, 0)),
                  kv_spec, kv_spec, front_spec, front_spec,
                  _resident(bias.shape), _resident(sink.shape),
                  pl.BlockSpec((tile, SWA_W), lambda b, t: (b * nt + t, 0))],
        out_specs=pl.BlockSpec((tile, SWA_W), lambda b, t: (b * nt + t, 0)),
        out_shape=jax.ShapeDtypeStruct((batch * seq, SWA_W), BF16),
        scratch_shapes=[pltpu.VMEM((SWA_KV_HEADS, 2, seq + WINDOW, SWA_PAIR_W), BF16)] * 2,
        compiler_params=_params(("parallel", "arbitrary")),
        name="swa",
    )(q, k, v, k_front, v_front, bias, sink, gate)


def _t5_bucket(rel):
    nb = NUM_BUCKETS // 2
    max_exact = nb // 2
    ret = np.where(rel > 0, nb, 0)
    n = np.abs(rel)
    nf = np.maximum(n, 1).astype(np.float32)
    large = max_exact + (np.log(nf / np.float32(max_exact)) / np.float32(math.log(MAX_DISTANCE / max_exact))
                         * np.float32(nb - max_exact)).astype(np.int32)
    large = np.minimum(large, nb - 1)
    return ret + np.where(n < max_exact, n, large)


def _band_bias(rel_bias):
    rel = np.arange(BAND)[None, :] - WINDOW - np.arange(CHUNK)[:, None]
    onehot = np.eye(NUM_BUCKETS, dtype=np.float32)[_t5_bucket(rel).reshape(-1)]
    b = jnp.dot(jnp.asarray(onehot), rel_bias.astype(F32), precision=lax.Precision.HIGHEST)
    b = b.reshape(CHUNK, BAND, SWA_KV_HEADS, SWA_PAIRS, 2)
    b = jnp.transpose(b, (2, 4, 1, 3, 0)).reshape(SWA_KV_HEADS, 2 * BAND, SWA_COLS)
    first_valid = np.array([WINDOW - v * CHUNK for v in range(SWA_BIAS_VARIANTS)])
    key = np.tile(np.arange(BAND), 2)
    mask = np.where(key[None, :] < first_valid[:, None], NEG_INF, 0.0).astype(np.float32)
    return b[None] + jnp.asarray(mask)[:, None, :, None]


def _sink_cols(sink):
    s = jnp.transpose(sink.astype(F32).reshape(SWA_KV_HEADS, SWA_PAIRS, 2), (0, 2, 1))
    return jnp.repeat(s, CHUNK, axis=-1).reshape(SWA_KV_HEADS, 2, 1, SWA_COLS)


MEM_SCALE = MEM_HD ** -0.5


def _mem_attn_kernel(q_ref, k_ref, v_ref, gate_ref, o_ref):
    cols = [slice(h * MEM_HD, (h + 1) * MEM_HD) for h in range(MEM_HEADS)]
    ones = jnp.ones((MEM_LEN, MEM_HD), BF16)
    s = [_dot_nt(q_ref[:, cs], k_ref[0, :, cs].astype(BF16)) * MEM_SCALE for cs in cols]
    p = [jnp.exp(x - jnp.max(x, axis=-1, keepdims=True)).astype(BF16) for x in s]
    pv = [_dot(x, jnp.concatenate([v_ref[0, :, cs].astype(BF16), ones], axis=1)) for x, cs in zip(p, cols)]
    for x, cs in zip(pv, cols):
        o_ref[:, cs] = (x[:, :MEM_HD] * (1.0 / x[:, MEM_HD:]) * gate_ref[:, cs]).astype(o_ref.dtype)


def _mem_attn(qm, mk, mv, gate, layer, batch, seq, tq):
    nt = seq // tq
    gate_block = gate.shape[1] // MEM_W - 1
    kv_spec = pl.BlockSpec((1, MEM_LEN, MEM_W), lambda b, t: (layer, b, 0))
    return pl.pallas_call(
        _mem_attn_kernel,
        grid=(batch, nt),
        in_specs=[pl.BlockSpec((tq, MEM_W), lambda b, t: (b * nt + t, 0)), kv_spec, kv_spec,
                  pl.BlockSpec((tq, MEM_W), lambda b, t: (b * nt + t, gate_block))],
        out_specs=pl.BlockSpec((tq, MEM_W), lambda b, t: (b * nt + t, 0)),
        out_shape=jax.ShapeDtypeStruct((batch * seq, MEM_W), BF16),
        compiler_params=_params(("parallel", "parallel")),
        name="mem_attn",
    )(qm, mk, mv, gate)


OUT_PROJ_SUBTILES = 2


def _out_proj_kernel(u_ref, um_ref, x_ref, w_ref, g_ref, y_ref, *, mix_w):
    rows = u_ref.shape[0] // OUT_PROJ_SUBTILES
    subs = [slice(r * rows, (r + 1) * rows) for r in range(OUT_PROJ_SUBTILES)]
    y = [_dot(u_ref[rs, :], w_ref[:mix_w, :]) + _dot(um_ref[rs, :], w_ref[mix_w:, :]) for rs in subs]
    for rs, yy in zip(subs, y):
        y_ref[rs, :] = x_ref[rs, :] + _rmsnorm(yy, g_ref[...])


def _out_proj(u, um, x, w, gain, tm):
    n_tok, d = x.shape
    mix_w = u.shape[1]
    row = lambda width: pl.BlockSpec((tm, width), lambda i: (i, 0))
    return pl.pallas_call(
        functools.partial(_out_proj_kernel, mix_w=mix_w),
        grid=(n_tok // tm,),
        in_specs=[row(mix_w), row(MEM_W), row(d), _resident(w.shape), _resident((1, d))],
        out_specs=row(d),
        out_shape=jax.ShapeDtypeStruct((n_tok, d), F32),
        compiler_params=_params(("parallel",)),
        name="out_proj",
    )(u, um, x, w, gain.reshape(1, d))


def _split_cols(w, widths):
    out, off = [], 0
    for width in widths:
        out.append(w[:, off:off + width].astype(BF16))
        off += width
    return out


@jax.jit
def kernel(x_prompt, x_sample, mem_prompt, cache_sb_k, cache_sb_v, cache_swa_k, cache_swa_v,
           cache_mem_k, cache_mem_v, pre_norm, post_norm, mem_norm, w_in_a, w_in_b, w_mem_kv,
           w_out, rel_bias, sinks):
    batch, seq, d = x_prompt.shape
    dec_batch, t_new, _ = x_sample.shape
    depth = pre_norm.shape[0]
    past = cache_sb_k.shape[2]
    n_p, n_s = batch * seq, dec_batch * t_new
    d_inner = w_out.shape[1]

    xp = x_prompt.reshape(n_p, d)
    xs = x_sample.reshape(n_s, d)
    proj_dtypes = (BF16, F32, F32, BF16, BF16)
    gate = (4,)
    widths_a = (SB_W, SB_W, SB_W, MEM_W, d_inner)
    widths_b = (SWA_W, SWA_KV_W, SWA_KV_W, MEM_W, d_inner)

    mem_k, mem_v = _mem_proj(mem_prompt.reshape(batch * MEM_LEN, d), mem_norm,
                             w_mem_kv.astype(BF16), tm=512)
    bias = _band_bias(rel_bias)
    zero_front = jnp.zeros((batch * WINDOW, SWA_KV_W), F32)

    n_a, n_b = (depth + 1) // 2, depth // 2
    sb_state_p = None
    heads_major = (0, 1, 3, 2, 4)
    cache_k = jnp.transpose(cache_sb_k, heads_major)
    cache_v = jnp.transpose(cache_sb_v, heads_major)
    cache_mk = cache_mem_k.reshape(depth, dec_batch * MEM_LEN, MEM_W)
    cache_mv = cache_mem_v.reshape(depth, dec_batch * MEM_LEN, MEM_W)
    sb_ks, sb_vs = [], []
    swa_kp, swa_vp, swa_ks, swa_vs = [], [], [], []
    for i in range(depth):
        j = i // 2
        if i % 2 == 0:
            weights = _split_cols(w_in_a[j], widths_a)
            (q_p, k_p, v_p, qm_p, z_p), sb_state_p = _norm_proj(
                xp, pre_norm[i], weights, (BF16, BF16, BF16, BF16, BF16), tm=512, gate_idx=gate,
                state_idx=(1, 2), slab=j, n_slabs=n_a, seq=seq, prev_states=sb_state_p)
            (q_s, k_s, v_s, qm_s, z_s), _ = _norm_proj(xs, pre_norm[i], weights, proj_dtypes, tm=256,
                                                       gate_idx=gate)
            o_p = _sb_prompt(q_p, k_p, v_p, z_p, batch, seq)
            o_s = _sb_sample(q_s, k_s, v_s, cache_k, cache_v, z_s, j, t_new)
            sb_ks.append(k_s)
            sb_vs.append(v_s)
        else:
            weights = _split_cols(w_in_b[j], widths_b)
            (q_p, k_p, v_p, qm_p, z_p), _ = _norm_proj(xp, pre_norm[i], weights, proj_dtypes, tm=1024,
                                                       gate_idx=gate)
            (q_s, k_s, v_s, qm_s, z_s), _ = _norm_proj(xs, pre_norm[i], weights, proj_dtypes, tm=256,
                                                       gate_idx=gate)
            sink = _sink_cols(sinks[j])
            o_p = _swa(q_p, k_p, v_p, zero_front, zero_front, bias, sink, z_p, batch, seq, front_valid=False)
            kc = cache_swa_k[j].reshape(dec_batch * WINDOW, SWA_KV_W)
            vc = cache_swa_v[j].reshape(dec_batch * WINDOW, SWA_KV_W)
            o_s = _swa(q_s, k_s, v_s, kc, vc, bias, sink, z_s, dec_batch, t_new, front_valid=True)
            swa_kp.append(k_p.reshape(batch, seq, SWA_KV_W)[:, seq - WINDOW:])
            swa_vp.append(v_p.reshape(batch, seq, SWA_KV_W)[:, seq - WINDOW:])
            k_all = jnp.concatenate([kc.reshape(dec_batch, WINDOW, SWA_KV_W),
                                     k_s.reshape(dec_batch, t_new, SWA_KV_W)], axis=1)
            v_all = jnp.concatenate([vc.reshape(dec_batch, WINDOW, SWA_KV_W),
                                     v_s.reshape(dec_batch, t_new, SWA_KV_W)], axis=1)
            swa_ks.append(k_all[:, t_new:])
            swa_vs.append(v_all[:, t_new:])
        om_p = _mem_attn(qm_p, mem_k, mem_v, z_p, i, batch, seq, tq=1024)
        om_s = _mem_attn(qm_s, cache_mk, cache_mv, z_s, i, dec_batch, t_new, tq=t_new)
        w_o = w_out[i].astype(BF16)
        xp = _out_proj(o_p, om_p, xp, w_o, post_norm[i], tm=1024)
        xs = _out_proj(o_s, om_s, xs, w_o, post_norm[i], tm=256)

    return (xp.reshape(batch, seq, d), xs.reshape(dec_batch, t_new, d),
            jnp.transpose(sb_state_p[0], heads_major),
            jnp.transpose(sb_state_p[1], heads_major),
            jnp.stack(sb_ks).reshape(n_a, dec_batch, t_new, SB_HEADS, SB_HD),
            jnp.stack(sb_vs).reshape(n_a, dec_batch, t_new, SB_HEADS, SB_HD),
            jnp.stack(swa_kp).reshape(n_b, batch, WINDOW, SWA_KV_HEADS, SWA_HD),
            jnp.stack(swa_vp).reshape(n_b, batch, WINDOW, SWA_KV_HEADS, SWA_HD),
            jnp.stack(swa_ks).reshape(n_b, dec_batch, WINDOW, SWA_KV_HEADS, SWA_HD),
            jnp.stack(swa_vs).reshape(n_b, dec_batch, WINDOW, SWA_KV_HEADS, SWA_HD),
            mem_k.reshape(depth, batch, MEM_LEN, MEM_HEADS, MEM_HD),
            mem_v.reshape(depth, batch, MEM_LEN, MEM_HEADS, MEM_HD))
```

```python
import functools
import math

import numpy as np
import jax
import jax.numpy as jnp
from jax import lax
from jax.experimental import pallas as pl
from jax.experimental.pallas import tpu as pltpu

F32 = jnp.float32
BF16 = jnp.bfloat16

D_MODEL = 1024
CHUNK = 64
MEM_LEN = 256
MEM_HEADS = 4
MEM_HD = 128
MEM_W = MEM_HEADS * MEM_HD
SB_HEADS = 12
SB_HD = 128
SB_W = SB_HEADS * SB_HD
SWA_HEADS = 24
SWA_KV_HEADS = 3
SWA_GROUP = SWA_HEADS // SWA_KV_HEADS
SWA_HD = 64
SWA_W = SWA_HEADS * SWA_HD
SWA_KV_W = SWA_KV_HEADS * SWA_HD
WINDOW = 128
BAND = WINDOW + CHUNK
NUM_BUCKETS = 32
MAX_DISTANCE = 128
EPS = 1e-6
NEG_INF = -1e30

VMEM_LIMIT_BYTES = 56 * 1024 * 1024
VMEM_LIMIT_FRESH_STATE_BYTES = 60 * 1024 * 1024


def _params(semantics, vmem_limit_bytes=VMEM_LIMIT_BYTES):
    return pltpu.CompilerParams(dimension_semantics=semantics, vmem_limit_bytes=vmem_limit_bytes)


def _resident(shape):
    nd = len(shape)
    return pl.BlockSpec(shape, lambda *_: (0,) * nd, pipeline_mode=pl.Buffered(1))


def _rmsnorm(x, g):
    return x * lax.rsqrt(jnp.mean(x * x, axis=-1, keepdims=True) + EPS) * g


def _dot(a, b):
    return jnp.dot(a, b, preferred_element_type=F32)


def _dot_nt(a, b):
    return lax.dot_general(a, b, (((1,), (1,)), ((), ())), preferred_element_type=F32)


PROJ_COL_CHUNK = 512
PROJ_SUBTILES = 2


def _silu(z):
    return z * (1.0 / (1.0 + jnp.exp(-z)))


def _norm_proj_kernel(x_ref, g_ref, *refs, widths, state_idx, gate_idx, n_prev, slab):
    n = len(widths)
    w_refs = refs[:n]
    o_refs = refs[n + n_prev:2 * n + n_prev]
    s_refs = refs[2 * n + n_prev:]
    s_slab = 0 if n_prev else slab
    rows = x_ref.shape[0] // PROJ_SUBTILES
    subs = [slice(r * rows, (r + 1) * rows) for r in range(PROJ_SUBTILES)]
    h = [_rmsnorm(x_ref[rs, :], g_ref[...]).astype(BF16) for rs in subs]
    for idx, (w_ref, o_ref, width) in enumerate(zip(w_refs, o_refs, widths)):
        s_ref = s_refs[state_idx.index(idx)] if idx in state_idx else None
        for c in range(0, width, PROJ_COL_CHUNK):
            cw = min(PROJ_COL_CHUNK, width - c)
            for rs, hs in zip(subs, h):
                y = _dot(hs, w_ref[:, c:c + cw])
                o_ref[rs, c:c + cw] = (_silu(y) if idx in gate_idx else y).astype(o_ref.dtype)
                if s_ref is not None:
                    for hh in range(cw // SB_HD):
                        s_ref[s_slab, 0, c // SB_HD + hh, rs, :] = y[:, hh * SB_HD:(hh + 1) * SB_HD]
    for s_ref in s_refs:
        for other in range(s_ref.shape[0]):
            if other != s_slab:
                s_ref[other] = jnp.zeros(s_ref.shape[1:], s_ref.dtype)


def _norm_proj(x, gain, weights, dtypes, tm, gate_idx=(), state_idx=(), slab=0, n_slabs=1, seq=None,
               prev_states=None):
    n_tok, d = x.shape
    widths = tuple(w.shape[1] for w in weights)
    prev = tuple(prev_states) if prev_states is not None else ()
    n_in = 2 + len(weights)
    state_specs, state_shapes = [], []
    if state_idx:
        nt = seq // tm
        if prev:
            spec = pl.BlockSpec((1, 1, SB_HEADS, tm, SB_HD), lambda i: (slab, i // nt, 0, i % nt, 0))
        else:
            spec = pl.BlockSpec((n_slabs, 1, SB_HEADS, tm, SB_HD), lambda i: (0, i // nt, 0, i % nt, 0))
        state_specs = [spec for _ in state_idx]
        state_shapes = [jax.ShapeDtypeStruct((n_slabs, n_tok // seq, SB_HEADS, seq, SB_HD), F32)
                        for _ in state_idx]
    outs = pl.pallas_call(
        functools.partial(_norm_proj_kernel, widths=widths, state_idx=tuple(state_idx),
                          gate_idx=tuple(gate_idx), n_prev=len(prev), slab=slab),
        grid=(n_tok // tm,),
        in_specs=[pl.BlockSpec((tm, d), lambda i: (i, 0)), _resident((1, d))]
        + [_resident(w.shape) for w in weights]
        + [pl.BlockSpec(memory_space=pl.ANY) for _ in prev],
        out_specs=[pl.BlockSpec((tm, w), lambda i: (i, 0)) for w in widths] + state_specs,
        out_shape=[jax.ShapeDtypeStruct((n_tok, w), dt) for w, dt in zip(widths, dtypes)] + state_shapes,
        input_output_aliases={n_in + s: len(widths) + s for s in range(len(prev))},
        compiler_params=_params(("parallel",), VMEM_LIMIT_FRESH_STATE_BYTES if state_idx and not prev
                                else VMEM_LIMIT_BYTES),
        name="norm_proj",
    )(x, gain.reshape(1, d), *weights, *prev)
    return outs[:len(widths)], outs[len(widths):]


def _mem_proj_kernel(x_ref, g_ref, w_ref, k_ref, v_ref):
    h = _rmsnorm(x_ref[...], g_ref[0]).astype(BF16)
    k_ref[0] = _dot(h, w_ref[0, :, :MEM_W])
    v_ref[0] = _dot(h, w_ref[0, :, MEM_W:])


def _mem_proj(mem, gains, w, tm):
    n_tok, d = mem.shape
    depth = w.shape[0]
    out = jax.ShapeDtypeStruct((depth, n_tok, MEM_W), F32)
    return pl.pallas_call(
        _mem_proj_kernel,
        grid=(depth, n_tok // tm),
        in_specs=[pl.BlockSpec((tm, d), lambda l, i: (i, 0)),
                  pl.BlockSpec((1, 1, d), lambda l, i: (l, 0, 0)),
                  pl.BlockSpec((1, d, 2 * MEM_W), lambda l, i: (l, 0, 0))],
        out_specs=[pl.BlockSpec((1, tm, MEM_W), lambda l, i: (l, i, 0))] * 2,
        out_shape=[out, out],
        compiler_params=_params(("parallel", "parallel")),
        name="mem_proj",
    )(mem, gains.reshape(depth, 1, d), w)


LOG2E = math.log2(math.e)
SB_LOG2_SCALE = SB_HD ** -0.5 * LOG2E
SB_TQ = 256
SB_TK = 256
SB_HEADS_PER_STEP = 6
SB_SAMPLE_HEADS_PER_STEP = 6
SB_QBLOCKS_PER_STEP = 4
SB_LIVE_ROWS = 160
SB_DEAD_LOG2 = 151.0


def _later_key_matrix(n):
    later = lax.broadcasted_iota(jnp.int32, (n, n), 0) > lax.broadcasted_iota(jnp.int32, (n, n), 1)
    return jnp.where(later, 1.0, 0.0).astype(BF16)


def _strict_causal(n_rows, n_cols, first_row):
    return (lax.broadcasted_iota(jnp.int32, (n_rows, n_cols), 1)
            < lax.broadcasted_iota(jnp.int32, (n_rows, n_cols), 0) + first_row)


def _sb_visit(chains):
    z2 = [_dot_nt(c[0], c[1]) * SB_LOG2_SCALE for c in chains]
    nl = [jnp.maximum(z, 0.0) + jnp.log2(1.0 + jnp.exp2(-jnp.abs(z))) for z in z2]
    nl = [x if c[4] is None else jnp.where(c[4], x, 0.0) for x, c in zip(nl, chains)]
    ncarry = [c[5] + jnp.sum(x, axis=-1, keepdims=True) for x, c in zip(nl, chains)]
    head = [z - x - c[5] for z, x, c in zip(z2, nl, chains)]
    a = [jnp.exp2(t - _dot(x.astype(BF16), c[3])) for t, x, c in zip(head, nl, chains)]
    a = [x if c[4] is None else jnp.where(c[4], x, 0.0) for x, c in zip(a, chains)]
    acc = [c[6] + _dot(x.astype(BF16), c[2]) for x, c in zip(a, chains)]
    return tuple(zip(ncarry, acc))


def _sb_older_blocks(n_blocks, visit, nc_ref, acc_ref, live_rows=None):
    heads, total = nc_ref.shape[0], nc_ref.shape[1]

    def liveness():
        least = functools.reduce(jnp.minimum, [nc_ref[c] for c in range(heads)])
        if live_rows is None:
            return (jnp.min(least) < SB_DEAD_LOG2).astype(jnp.int32), jnp.int32(0)
        lead, tail = jnp.min(least[:live_rows]), jnp.min(least[live_rows:])
        return ((jnp.minimum(lead, tail) < SB_DEAD_LOG2).astype(jnp.int32),
                (tail >= SB_DEAD_LOG2).astype(jnp.int32))

    def step(block, rows):
        new = visit(block, rows, tuple((nc_ref[c, :rows], acc_ref[c, :rows]) for c in range(heads)))
        _sb_store(nc_ref, acc_ref, new, rows)

    def cond(loop):
        return jnp.logical_and(loop[0] < n_blocks, loop[1] > 0)

    def body(loop):
        block, _, tail_dead = loop
        if live_rows is None:
            step(block, total)
        else:
            pl.when(tail_dead > 0)(lambda: step(block, live_rows))
            pl.when(tail_dead == 0)(lambda: step(block, total))
        return (block + 1,) + liveness()

    return lax.while_loop(cond, body, (jnp.int32(0),) + liveness())[0]


def _sb_store(nc_ref, acc_ref, state, rows):
    for c, (ncarry, acc) in enumerate(state):
        nc_ref[c, :rows] = ncarry
        acc_ref[c, :rows] = acc


def _sb_zero(rows):
    return jnp.zeros((rows, 1), F32), jnp.zeros((rows, SB_HD), F32)


def _sb_prompt_kernel(q_ref, k_ref, v_ref, gate_ref, o_ref, nc_ref, acc_ref, *, heads):
    cols = [slice(c * SB_HD, (c + 1) * SB_HD) for c in range(heads)]
    later = _later_key_matrix(SB_TK)
    mask = _strict_causal(SB_TQ, SB_TK, 0)
    for sub in range(SB_QBLOCKS_PER_STEP):
        _sb_prompt_block(q_ref, k_ref, v_ref, gate_ref, o_ref, nc_ref, acc_ref, cols, later, mask,
                         pl.program_id(2) * SB_QBLOCKS_PER_STEP + sub, sub * SB_TQ)


def _sb_prompt_block(q_ref, k_ref, v_ref, gate_ref, o_ref, nc_ref, acc_ref, cols, later, mask, i, row0):
    diag = pl.ds(pl.multiple_of(i * SB_TK, SB_TK), SB_TK)
    _sb_store(nc_ref, acc_ref,
              _sb_visit([(q_ref[row0:row0 + SB_TQ, cs], k_ref[diag, cs], v_ref[diag, cs], later, mask)
                         + _sb_zero(SB_TQ) for cs in cols]), SB_TQ)

    def older(jj, rows, st):
        keys = pl.ds(pl.multiple_of((i - 1 - jj) * SB_TK, SB_TK), SB_TK)
        return _sb_visit([(q_ref[row0:row0 + rows, cs], k_ref[keys, cs], v_ref[keys, cs], later, None) + tuple(s)
                          for cs, s in zip(cols, st)])

    _sb_older_blocks(i, older, nc_ref, acc_ref, live_rows=SB_LIVE_ROWS)
    rows = slice(row0, row0 + SB_TQ)
    for c, cs in enumerate(cols):
        o_ref[rows, cs] = (acc_ref[c] * gate_ref[rows, cs]).astype(o_ref.dtype)


def _sb_prompt(q, k, v, gate, batch, seq):
    n_tok = batch * seq
    rows = SB_TQ * SB_QBLOCKS_PER_STEP
    nq = seq // rows
    heads = SB_HEADS_PER_STEP
    width = heads * SB_HD
    return pl.pallas_call(
        functools.partial(_sb_prompt_kernel, heads=heads),
        grid=(batch, SB_HEADS // heads, nq),
        in_specs=[pl.BlockSpec((rows, width), lambda b, h, i: (b * nq + i, h)),
                  pl.BlockSpec((seq, width), lambda b, h, i: (b, h)),
                  pl.BlockSpec((seq, width), lambda b, h, i: (b, h)),
                  pl.BlockSpec((rows, width), lambda b, h, i: (b * nq + i, h))],
        out_specs=pl.BlockSpec((rows, width), lambda b, h, i: (b * nq + i, h)),
        out_shape=jax.ShapeDtypeStruct((n_tok, SB_W), BF16),
        scratch_shapes=[pltpu.VMEM((heads, SB_TQ, 1), F32), pltpu.VMEM((heads, SB_TQ, SB_HD), F32)],
        compiler_params=_params(("parallel", "parallel", "parallel")),
        name="sb_prompt",
    )(q, k, v, gate)


def _sb_sample_kernel(q_ref, kn_ref, vn_ref, kc_hbm, vc_hbm, gate_ref, o_ref, nc_ref, acc_ref, kbuf, vbuf, ksem, vsem,
                      *, heads, t_new, past, layer):
    b, hg = pl.program_id(0), pl.program_id(1)
    cols = [slice(c * SB_HD, (c + 1) * SB_HD) for c in range(heads)]
    n_blocks = past // SB_TK

    def copies(block, slot):
        keys = pl.ds(pl.multiple_of((n_blocks - 1 - block) * SB_TK, SB_TK), SB_TK)
        src = lambda ref: ref.at[layer, b, pl.ds(hg * heads, heads), keys, :]
        return (pltpu.make_async_copy(src(kc_hbm), kbuf.at[slot], ksem.at[slot]),
                pltpu.make_async_copy(src(vc_hbm), vbuf.at[slot], vsem.at[slot]))

    def start(block, slot):
        for cp in copies(block, slot):
            cp.start()

    def wait(block, slot):
        for cp in copies(block, slot):
            cp.wait()

    start(0, 0)
    later_new, mask_new = _later_key_matrix(t_new), _strict_causal(t_new, t_new, 0)
    _sb_store(nc_ref, acc_ref,
              _sb_visit([(q_ref[:, cs], kn_ref[:, cs].astype(BF16), vn_ref[:, cs].astype(BF16), later_new, mask_new)
                         + _sb_zero(t_new) for cs in cols]), t_new)
    later = _later_key_matrix(SB_TK)

    def older(jj, rows, st):
        slot = lax.rem(jj, 2)
        wait(jj, slot)
        pl.when(jj + 1 < n_blocks)(lambda: start(jj + 1, 1 - slot))
        return _sb_visit([(q_ref[:, cs], kbuf[slot, c].astype(BF16), vbuf[slot, c].astype(BF16), later, None) + tuple(s)
                          for c, (cs, s) in enumerate(zip(cols, st))])

    visited = _sb_older_blocks(n_blocks, older, nc_ref, acc_ref)
    pl.when(visited < n_blocks)(lambda: wait(visited, lax.rem(visited, 2)))
    for c, cs in enumerate(cols):
        o_ref[:, cs] = (acc_ref[c] * gate_ref[:, cs]).astype(o_ref.dtype)


def _sb_sample(q, k_new, v_new, k_cache, v_cache, gate, layer, t_new):
    _, batch, _, past, _ = k_cache.shape
    heads = SB_SAMPLE_HEADS_PER_STEP
    width = heads * SB_HD
    new_spec = pl.BlockSpec((t_new, width), lambda b, h: (b, h))
    cache_spec = pl.BlockSpec(memory_space=pl.ANY)
    block_buf = pltpu.VMEM((2, heads, SB_TK, SB_HD), F32)
    return pl.pallas_call(
        functools.partial(_sb_sample_kernel, heads=heads, t_new=t_new, past=past, layer=layer),
        grid=(batch, SB_HEADS // heads),
        in_specs=[new_spec, new_spec, new_spec, cache_spec, cache_spec, new_spec],
        out_specs=new_spec,
        out_shape=jax.ShapeDtypeStruct((batch * t_new, SB_W), BF16),
        scratch_shapes=[pltpu.VMEM((heads, t_new, 1), F32), pltpu.VMEM((heads, t_new, SB_HD), F32),
                        block_buf, block_buf, pltpu.SemaphoreType.DMA((2,)), pltpu.SemaphoreType.DMA((2,))],
        compiler_params=_params(("arbitrary", "arbitrary")),
        name="sb_sample",
    )(q, k_new, v_new, k_cache, v_cache, gate)


SWA_SCALE = SWA_HD ** -0.5
SWA_PAIRS = SWA_GROUP // 2
SWA_PAIR_W = 2 * SWA_HD
SWA_COLS = SWA_PAIRS * CHUNK
SWA_BIAS_VARIANTS = 3
SWA_CHUNKS_PER_STEP = 8


def _swa_fill(dst_ref, front_ref, body_ref, seq):
    for g in range(SWA_KV_HEADS):
        for (lo, hi), src_ref in (((0, WINDOW), front_ref), ((WINDOW, WINDOW + seq), body_ref)):
            x = src_ref[:, g * SWA_HD:(g + 1) * SWA_HD]
            zero = jnp.zeros_like(x)
            dst_ref[g, 0, lo:hi, :] = jnp.concatenate([x, zero], axis=1).astype(BF16)
            dst_ref[g, 1, lo:hi, :] = jnp.concatenate([zero, x], axis=1).astype(BF16)


def _swa_softmax(s, sk):
    m = jnp.maximum(jnp.max(s, axis=0, keepdims=True), sk)
    p = jnp.exp(s - m)
    den = jnp.sum(p, axis=0, keepdims=True) + jnp.exp(sk - m)
    return p.astype(BF16), 1.0 / den


def _swa_kernel(q_ref, k_ref, v_ref, kf_ref, vf_ref, bias_ref, sink_ref, gate_ref, o_ref, kab_ref, vab_ref,
                *, chunks, seq, front_valid):
    t = pl.program_id(1)

    @pl.when(t == 0)
    def _():
        _swa_fill(kab_ref, kf_ref, k_ref, seq)
        _swa_fill(vab_ref, vf_ref, v_ref, seq)

    blocks = [(c, g) for c in range(chunks) for g in range(SWA_KV_HEADS)]

    def band(ref, c, g):
        rows = pl.ds(pl.multiple_of((t * chunks + c) * CHUNK, CHUNK), BAND)
        return jnp.concatenate([ref[g, 0, rows, :], ref[g, 1, rows, :]], axis=0)

    def q_pairs(c, g):
        return jnp.concatenate(
            [q_ref[c * CHUNK:(c + 1) * CHUNK, (g * SWA_PAIRS + r) * SWA_PAIR_W:(g * SWA_PAIRS + r + 1) * SWA_PAIR_W]
             for r in range(SWA_PAIRS)], axis=0)

    def variant(c):
        return SWA_BIAS_VARIANTS - 1 if front_valid else jnp.minimum(t * chunks + c, SWA_BIAS_VARIANTS - 1)

    s = [_dot_nt(band(kab_ref, c, g), q_pairs(c, g)) * SWA_SCALE + bias_ref[variant(c), g] for c, g in blocks]
    p = [(_swa_softmax(x[:BAND], sink_ref[g, 0]), _swa_softmax(x[BAND:], sink_ref[g, 1])) for x, (c, g) in zip(s, blocks)]
    pv = [lax.dot_general(band(vab_ref, c, g), jnp.concatenate([even[0], odd[0]], axis=0),
                          (((0,), (0,)), ((), ())), preferred_element_type=F32)
          for (even, odd), (c, g) in zip(p, blocks)]
    o = [jnp.concatenate([x[:SWA_HD] * even[1], x[SWA_HD:] * odd[1]], axis=0).T
         for x, (even, odd) in zip(pv, p)]
    for x, (c, g) in zip(o, blocks):
        for r in range(SWA_PAIRS):
            lanes = slice((g * SWA_PAIRS + r) * SWA_PAIR_W, (g * SWA_PAIRS + r + 1) * SWA_PAIR_W)
            rows = slice(c * CHUNK, (c + 1) * CHUNK)
            o_ref[rows, lanes] = (x[r * CHUNK:(r + 1) * CHUNK] * gate_ref[rows, lanes]).astype(o_ref.dtype)


def _swa(q, k, v, k_front, v_front, bias, sink, gate, batch, seq, front_valid):
    chunks = min(SWA_CHUNKS_PER_STEP, seq // CHUNK)
    tile = chunks * CHUNK
    nt = seq // tile
    kv_spec = pl.BlockSpec((seq, SWA_KV_W), lambda b, t: (b, 0))
    front_spec = pl.BlockSpec((WINDOW, SWA_KV_W), lambda b, t: (b, 0))
    return pl.pallas_call(
        functools.partial(_swa_kernel, chunks=chunks, seq=seq, front_valid=front_valid),
        grid=(batch, nt),
        in_specs=[pl.BlockSpec((tile, SWA_W), lambda b, t: (b * nt + t, 0)),
                  kv_spec, kv_spec, front_spec, front_spec,
                  _resident(bias.shape), _resident(sink.shape),
                  pl.BlockSpec((tile, SWA_W), lambda b, t: (b * nt + t, 0))],
        out_specs=pl.BlockSpec((tile, SWA_W), lambda b, t: (b * nt + t, 0)),
        out_shape=jax.ShapeDtypeStruct((batch * seq, SWA_W), BF16),
        scratch_shapes=[pltpu.VMEM((SWA_KV_HEADS, 2, seq + WINDOW, SWA_PAIR_W), BF16)] * 2,
        compiler_params=_params(("parallel", "arbitrary")),
        name="swa",
    )(q, k, v, k_front, v_front, bias, sink, gate)


def _t5_bucket(rel):
    nb = NUM_BUCKETS // 2
    max_exact = nb // 2
    ret = np.where(rel > 0, nb, 0)
    n = np.abs(rel)
    nf = np.maximum(n, 1).astype(np.float32)
    large = max_exact + (np.log(nf / np.float32(max_exact)) / np.float32(math.log(MAX_DISTANCE / max_exact))
                         * np.float32(nb - max_exact)).astype(np.int32)
    large = np.minimum(large, nb - 1)
    return ret + np.where(n < max_exact, n, large)


def _band_bias(rel_bias):
    rel = np.arange(BAND)[None, :] - WINDOW - np.arange(CHUNK)[:, None]
    onehot = np.eye(NUM_BUCKETS, dtype=np.float32)[_t5_bucket(rel).reshape(-1)]
    b = jnp.dot(jnp.asarray(onehot), rel_bias.astype(F32), precision=lax.Precision.HIGHEST)
    b = b.reshape(CHUNK, BAND, SWA_KV_HEADS, SWA_PAIRS, 2)
    b = jnp.transpose(b, (2, 4, 1, 3, 0)).reshape(SWA_KV_HEADS, 2 * BAND, SWA_COLS)
    first_valid = np.array([WINDOW - v * CHUNK for v in range(SWA_BIAS_VARIANTS)])
    key = np.tile(np.arange(BAND), 2)
    mask = np.where(key[None, :] < first_valid[:, None], NEG_INF, 0.0).astype(np.float32)
    return b[None] + jnp.asarray(mask)[:, None, :, None]


def _sink_cols(sink):
    s = jnp.transpose(sink.astype(F32).reshape(SWA_KV_HEADS, SWA_PAIRS, 2), (0, 2, 1))
    return jnp.repeat(s, CHUNK, axis=-1).reshape(SWA_KV_HEADS, 2, 1, SWA_COLS)


MEM_SCALE = MEM_HD ** -0.5


def _mem_attn_kernel(q_ref, k_ref, v_ref, gate_ref, o_ref):
    cols = [slice(h * MEM_HD, (h + 1) * MEM_HD) for h in range(MEM_HEADS)]
    ones = jnp.ones((MEM_LEN, MEM_HD), BF16)
    s = [_dot_nt(q_ref[:, cs], k_ref[0, :, cs].astype(BF16)) * MEM_SCALE for cs in cols]
    p = [jnp.exp(x - jnp.max(x, axis=-1, keepdims=True)).astype(BF16) for x in s]
    pv = [_dot(x, jnp.concatenate([v_ref[0, :, cs].astype(BF16), ones], axis=1)) for x, cs in zip(p, cols)]
    for x, cs in zip(pv, cols):
        o_ref[:, cs] = (x[:, :MEM_HD] * (1.0 / x[:, MEM_HD:]) * gate_ref[:, cs]).astype(o_ref.dtype)


def _mem_attn(qm, mk, mv, gate, layer, batch, seq, tq):
    nt = seq // tq
    gate_block = gate.shape[1] // MEM_W - 1
    kv_spec = pl.BlockSpec((1, MEM_LEN, MEM_W), lambda b, t: (layer, b, 0))
    return pl.pallas_call(
        _mem_attn_kernel,
        grid=(batch, nt),
        in_specs=[pl.BlockSpec((tq, MEM_W), lambda b, t: (b * nt + t, 0)), kv_spec, kv_spec,
                  pl.BlockSpec((tq, MEM_W), lambda b, t: (b * nt + t, gate_block))],
        out_specs=pl.BlockSpec((tq, MEM_W), lambda b, t: (b * nt + t, 0)),
        out_shape=jax.ShapeDtypeStruct((batch * seq, MEM_W), BF16),
        compiler_params=_params(("parallel", "parallel")),
        name="mem_attn",
    )(qm, mk, mv, gate)


OUT_PROJ_SUBTILES = 2


def _out_proj_kernel(u_ref, um_ref, x_ref, w_ref, g_ref, y_ref, *, mix_w):
    rows = u_ref.shape[0] // OUT_PROJ_SUBTILES
    subs = [slice(r * rows, (r + 1) * rows) for r in range(OUT_PROJ_SUBTILES)]
    y = [_dot(u_ref[rs, :], w_ref[:mix_w, :]) + _dot(um_ref[rs, :], w_ref[mix_w:, :]) for rs in subs]
    for rs, yy in zip(subs, y):
        y_ref[rs, :] = x_ref[rs, :] + _rmsnorm(yy, g_ref[...])


def _out_proj(u, um, x, w, gain, tm):
    n_tok, d = x.shape
    mix_w = u.shape[1]
    row = lambda width: pl.BlockSpec((tm, width), lambda i: (i, 0))
    return pl.pallas_call(
        functools.partial(_out_proj_kernel, mix_w=mix_w),
        grid=(n_tok // tm,),
        in_specs=[row(mix_w), row(MEM_W), row(d), _resident(w.shape), _resident((1, d))],
        out_specs=row(d),
        out_shape=jax.ShapeDtypeStruct((n_tok, d), F32),
        compiler_params=_params(("parallel",)),
        name="out_proj",
    )(u, um, x, w, gain.reshape(1, d))


def _split_cols(w, widths):
    out, off = [], 0
    for width in widths:
        out.append(w[:, off:off + width].astype(BF16))
        off += width
    return out


@jax.jit
def kernel(x_prompt, x_sample, mem_prompt, cache_sb_k, cache_sb_v, cache_swa_k, cache_swa_v,
           cache_mem_k, cache_mem_v, pre_norm, post_norm, mem_norm, w_in_a, w_in_b, w_mem_kv,
           w_out, rel_bias, sinks):
    batch, seq, d = x_prompt.shape
    dec_batch, t_new, _ = x_sample.shape
    depth = pre_norm.shape[0]
    past = cache_sb_k.shape[2]
    n_p, n_s = batch * seq, dec_batch * t_new
    d_inner = w_out.shape[1]

    xp = x_prompt.reshape(n_p, d)
    xs = x_sample.reshape(n_s, d)
    proj_dtypes = (BF16, F32, F32, BF16, BF16)
    gate = (4,)
    widths_a = (SB_W, SB_W, SB_W, MEM_W, d_inner)
    widths_b = (SWA_W, SWA_KV_W, SWA_KV_W, MEM_W, d_inner)

    mem_k, mem_v = _mem_proj(mem_prompt.reshape(batch * MEM_LEN, d), mem_norm,
                             w_mem_kv.astype(BF16), tm=512)
    bias = _band_bias(rel_bias)
    zero_front = jnp.zeros((batch * WINDOW, SWA_KV_W), F32)

    n_a, n_b = (depth + 1) // 2, depth // 2
    sb_state_p = None
    heads_major = (0, 1, 3, 2, 4)
    cache_k = jnp.transpose(cache_sb_k, heads_major)
    cache_v = jnp.transpose(cache_sb_v, heads_major)
    cache_mk = cache_mem_k.reshape(depth, dec_batch * MEM_LEN, MEM_W)
    cache_mv = cache_mem_v.reshape(depth, dec_batch * MEM_LEN, MEM_W)
    sb_ks, sb_vs = [], []
    swa_kp, swa_vp, swa_ks, swa_vs = [], [], [], []
    for i in range(depth):
        j = i // 2
        if i % 2 == 0:
            weights = _split_cols(w_in_a[j], widths_a)
            (q_p, k_p, v_p, qm_p, z_p), sb_state_p = _norm_proj(
                xp, pre_norm[i], weights, (BF16, BF16, BF16, BF16, BF16), tm=512, gate_idx=gate,
                state_idx=(1, 2), slab=j, n_slabs=n_a, seq=seq, prev_states=sb_state_p)
            (q_s, k_s, v_s, qm_s, z_s), _ = _norm_proj(xs, pre_norm[i], weights, proj_dtypes, tm=256,
                                                       gate_idx=gate)
            o_p = _sb_prompt(q_p, k_p, v_p, z_p, batch, seq)
            o_s = _sb_sample(q_s, k_s, v_s, cache_k, cache_v, z_s, j, t_new)
            sb_ks.append(k_s)
            sb_vs.append(v_s)
        else:
            weights = _split_cols(w_in_b[j], widths_b)
            (q_p, k_p, v_p, qm_p, z_p), _ = _norm_proj(xp, pre_norm[i], weights, proj_dtypes, tm=512,
                                                       gate_idx=gate)
            (q_s, k_s, v_s, qm_s, z_s), _ = _norm_proj(xs, pre_norm[i], weights, proj_dtypes, tm=256,
                                                       gate_idx=gate)
            sink = _sink_cols(sinks[j])
            o_p = _swa(q_p, k_p, v_p, zero_front, zero_front, bias, sink, z_p, batch, seq, front_valid=False)
            kc = cache_swa_k[j].reshape(dec_batch * WINDOW, SWA_KV_W)
            vc = cache_swa_v[j].reshape(dec_batch * WINDOW, SWA_KV_W)
            o_s = _swa(q_s, k_s, v_s, kc, vc, bias, sink, z_s, dec_batch, t_new, front_valid=True)
            swa_kp.append(k_p.reshape(batch, seq, SWA_KV_W)[:, seq - WINDOW:])
            swa_vp.append(v_p.reshape(batch, seq, SWA_KV_W)[:, seq - WINDOW:])
            k_all = jnp.concatenate([kc.reshape(dec_batch, WINDOW, SWA_KV_W),
                                     k_s.reshape(dec_batch, t_new, SWA_KV_W)], axis=1)
            v_all = jnp.concatenate([vc.reshape(dec_batch, WINDOW, SWA_KV_W),
                                     v_s.reshape(dec_batch, t_new, SWA_KV_W)], axis=1)
            swa_ks.append(k_all[:, t_new:])
            swa_vs.append(v_all[:, t_new:])
        om_p = _mem_attn(qm_p, mem_k, mem_v, z_p, i, batch, seq, tq=1024)
        om_s = _mem_attn(qm_s, cache_mk, cache_mv, z_s, i, dec_batch, t_new, tq=t_new)
        w_o = w_out[i].astype(BF16)
        xp = _out_proj(o_p, om_p, xp, w_o, post_norm[i], tm=1024)
        xs = _out_proj(o_s, om_s, xs, w_o, post_norm[i], tm=256)

    return (xp.reshape(batch, seq, d), xs.reshape(dec_batch, t_new, d),
            jnp.transpose(sb_state_p[0], heads_major),
            jnp.transpose(sb_state_p[1], heads_major),
            jnp.stack(sb_ks).reshape(n_a, dec_batch, t_new, SB_HEADS, SB_HD),
            jnp.stack(sb_vs).reshape(n_a, dec_batch, t_new, SB_HEADS, SB_HD),
            jnp.stack(swa_kp).reshape(n_b, batch, WINDOW, SWA_KV_HEADS, SWA_HD),
            jnp.stack(swa_vp).reshape(n_b, batch, WINDOW, SWA_KV_HEADS, SWA_HD),
            jnp.stack(swa_ks).reshape(n_b, dec_batch, WINDOW, SWA_KV_HEADS, SWA_HD),
            jnp.stack(swa_vs).reshape(n_b, dec_batch, WINDOW, SWA_KV_HEADS, SWA_HD),
            mem_k.reshape(depth, batch, MEM_LEN, MEM_HEADS, MEM_HD),
            mem_v.reshape(depth, batch, MEM_LEN, MEM_HEADS, MEM_HD))
```
